```python
import jax, jax.numpy as jnp
from jax import lax
import numpy as np

D_MODEL = 1024
BATCH = 4
SEQ = 4096
DEPTH = 2

CHUNK = 64
Q_BLOCK = 128

ATT_HEADS = 8
ATT_HEAD_DIM = 64
ATT_WIDTH = ATT_HEADS * ATT_HEAD_DIM
IDX_HEADS = 8
IDX_HEAD_DIM = 64
TOPK_MAX = 256

POOL_GROUPS = 4
POOL_GROUP_DIM = 64
POOL_WIDTH = POOL_GROUPS * POOL_GROUP_DIM
POOL_WINDOWS = (2, 4, 8, 16)

CONV_WIDTH = 256
CONV_KERNEL = 31

MIX_WIDTH = ATT_WIDTH + POOL_WIDTH + CONV_WIDTH
IN_SIZES = (ATT_WIDTH, ATT_WIDTH, ATT_WIDTH,
            IDX_HEADS * IDX_HEAD_DIM, IDX_HEAD_DIM, IDX_HEADS,
            POOL_WIDTH,
            2 * CONV_WIDTH)
N_IN = sum(IN_SIZES)

N_GROUPS = 4
EXPERTS_PER_GROUP = 4
EXPERT_HIDDEN = 512
TOP_K_EXPERTS = 2

EPS = 1e-6

kernel_name = "hybrid_dsa_pool_conformer_hiermoe"


def rmsnorm(x, g):
    xf = x.astype(jnp.float32)
    y = xf * lax.rsqrt(jnp.mean(xf * xf, axis=-1, keepdims=True) + EPS)
    return (y * g.astype(jnp.float32)).astype(x.dtype)


def split_columns(p):
    offsets = []
    acc = 0
    for s in IN_SIZES[:-1]:
        acc += s
        offsets.append(acc)
    return jnp.split(p, offsets, axis=-1)


def dsa_attention(q, k, v, q_idx, k_idx, w_idx):
    B, S = q.shape[0], q.shape[1]
    nb = S // Q_BLOCK
    k_top = min(TOPK_MAX, S // 4)
    key_chunk = jnp.arange(S, dtype=jnp.int32) // CHUNK
    pos_blocks = jnp.arange(S, dtype=jnp.int32).reshape(nb, Q_BLOCK)
    k_idx_f = k_idx.astype(jnp.float32)
    att_scale = ATT_HEAD_DIM ** -0.5

    def to_blocks(a):
        return a.reshape((B, nb, Q_BLOCK) + a.shape[2:]).swapaxes(0, 1)

    def block_fn(args):
        qb, qib, wb, pos = args
        q_chunk = pos // CHUNK
        dots = jnp.einsum('bqhd,bsd->bqhs', qib.astype(jnp.float32), k_idx_f) * (IDX_HEAD_DIM ** -0.5)
        score = jnp.einsum('bqhs,bqh->bqs', jax.nn.relu(dots), wb.astype(jnp.float32)) * (IDX_HEADS ** -0.5)
        admissible = key_chunk[None, :] <= q_chunk[:, None]
        score = jnp.where(admissible[None], score, -jnp.inf)
        _, sel = lax.top_k(score, k_top)
        sel_ok = key_chunk[sel] <= q_chunk[None, :, None]
        k_sel = jax.vmap(lambda kb, ib: kb[ib])(k, sel)
        v_sel = jax.vmap(lambda vb, ib: vb[ib])(v, sel)
        logits = jnp.einsum('bqhd,bqkhd->bqhk', qb, k_sel).astype(jnp.float32) * att_scale
        logits = jnp.where(sel_ok[:, :, None, :], logits, -jnp.inf)
        p = jax.nn.softmax(logits, axis=-1).astype(v.dtype)
        return jnp.einsum('bqhk,bqkhd->bqhd', p, v_sel)

    out = lax.map(block_fn, (to_blocks(q), to_blocks(q_idx), to_blocks(w_idx), pos_blocks))
    return out.swapaxes(0, 1).reshape(B, S, ATT_HEADS * ATT_HEAD_DIM)


def pool_mixer(u, pool_w, pool_scale):
    B, S, _ = u.shape
    uf = u.astype(jnp.float32)
    csum = jnp.concatenate([jnp.zeros((B, 1, POOL_WIDTH), jnp.float32), jnp.cumsum(uf, axis=1)], axis=1)
    t1 = jnp.arange(1, S + 1, dtype=jnp.float32)
    pooled = []
    for gi, w in enumerate(POOL_WINDOWS):
        c = csum[..., gi * POOL_GROUP_DIM:(gi + 1) * POOL_GROUP_DIM]
        lagged = jnp.pad(c, ((0, 0), (w, 0), (0, 0)))[:, :S + 1]
        window_sum = c[:, 1:] - lagged[:, 1:]
        count = jnp.minimum(t1, jnp.float32(w))
        pooled.append(window_sum / count[None, :, None])
    d = (jnp.concatenate(pooled, axis=-1) - uf).astype(u.dtype)
    d = d.reshape(B, S, POOL_GROUPS, POOL_GROUP_DIM)
    y = jnp.einsum('bsgc,gcd->bsgd', d, pool_w).reshape(B, S, POOL_WIDTH)
    return y * pool_scale


def conformer_conv(u, dw_w, dw_b, ln_g, ln_b, pw_w, pw_b):
    a, gate = jnp.split(u, 2, axis=-1)
    h = a * jax.nn.sigmoid(gate)
    h = lax.conv_general_dilated(h, dw_w[:, None, :], window_strides=(1,),
                                 padding=[(CONV_KERNEL - 1, 0)],
                                 dimension_numbers=('NWC', 'WIO', 'NWC'),
                                 feature_group_count=CONV_WIDTH) + dw_b
    hf = h.astype(jnp.float32)
    mu = jnp.mean(hf, axis=-1, keepdims=True)
    var = jnp.mean(jnp.square(hf - mu), axis=-1, keepdims=True)
    hf = (hf - mu) * lax.rsqrt(var + EPS) * ln_g.astype(jnp.float32) + ln_b.astype(jnp.float32)
    h = jax.nn.silu(hf).astype(u.dtype)
    return h @ pw_w + pw_b


def hier_moe(h, rg_w, rg_b, re_w, re_b, w_gate, w_up, w_down):
    B, S, D = h.shape
    t = h.reshape(B * S, D)
    g_prob = jax.nn.softmax((t @ rg_w + rg_b).astype(jnp.float32), axis=-1)
    g_sel = jnp.argmax(g_prob, axis=-1)
    p_g = jnp.take_along_axis(g_prob, g_sel[:, None], axis=1)[:, 0]
    e_logits = (t @ re_w + re_b).astype(jnp.float32).reshape(-1, N_GROUPS, EXPERTS_PER_GROUP)
    e_in = jnp.take_along_axis(e_logits, g_sel[:, None, None], axis=1)[:, 0]
    e_prob = jax.nn.softmax(e_in, axis=-1)
    top_v, top_i = lax.top_k(e_prob, TOP_K_EXPERTS)
    top_v = top_v / jnp.sum(top_v, axis=-1, keepdims=True)
    w_e = jnp.sum(jax.nn.one_hot(top_i, EXPERTS_PER_GROUP, dtype=jnp.float32) * top_v[..., None], axis=1)
    gate = (jax.nn.one_hot(g_sel, N_GROUPS, dtype=jnp.float32)[:, :, None]
            * (p_g[:, None, None] * w_e[:, None, :])).astype(h.dtype)
    y = jnp.zeros_like(t)
    for gi in range(N_GROUPS):
        a = jnp.einsum('td,edh->teh', t, w_gate[gi])
        b = jnp.einsum('td,edh->teh', t, w_up[gi])
        act = jax.nn.silu(a) * b * gate[:, gi, :, None]
        y = y + jnp.einsum('teh,ehd->td', act, w_down[gi])
    return y.reshape(B, S, D)


def setup_inputs(seed: int = 0) -> dict:
    key = jax.random.key(seed)
    ks = jax.random.split(key, 24)
    f32 = jnp.float32
    n = lambda k, shape: jax.random.normal(k, shape, f32)
    L, D, G, E, H = DEPTH, D_MODEL, N_GROUPS, EXPERTS_PER_GROUP, EXPERT_HIDDEN
    return {
        "x": n(ks[0], (BATCH, SEQ, D)),
        "norm1_g": 1.0 + 0.02 * n(ks[1], (L, D)),
        "w_in": n(ks[2], (L, D, N_IN)) * D ** -0.5,
        "pool_w": n(ks[3], (L, POOL_GROUPS, POOL_GROUP_DIM, POOL_GROUP_DIM)) * POOL_GROUP_DIM ** -0.5,
        "pool_scale": 1.0 + 0.02 * n(ks[4], (L, POOL_WIDTH)),
        "dw_w": n(ks[5], (L, CONV_KERNEL, CONV_WIDTH)) * CONV_KERNEL ** -0.5,
        "dw_b": 0.02 * n(ks[6], (L, CONV_WIDTH)),
        "conv_ln_g": 1.0 + 0.02 * n(ks[7], (L, CONV_WIDTH)),
        "conv_ln_b": 0.02 * n(ks[8], (L, CONV_WIDTH)),
        "pw_w": n(ks[9], (L, CONV_WIDTH, CONV_WIDTH)) * CONV_WIDTH ** -0.5,
        "pw_b": 0.02 * n(ks[10], (L, CONV_WIDTH)),
        "w_out": n(ks[11], (L, MIX_WIDTH, D)) * MIX_WIDTH ** -0.5,
        "norm2_g": 1.0 + 0.02 * n(ks[12], (L, D)),
        "rg_w": n(ks[13], (L, D, G)) * D ** -0.5,
        "rg_b": 0.01 * n(ks[14], (L, G)),
        "re_w": n(ks[15], (L, D, G * E)) * D ** -0.5,
        "re_b": 0.01 * n(ks[16], (L, G * E)),
        "w_gate": n(ks[17], (L, G, E, D, H)) * D ** -0.5,
        "w_up": n(ks[18], (L, G, E, D, H)) * D ** -0.5,
        "w_down": n(ks[19], (L, G, E, H, D)) * H ** -0.5,
        "final_g": 1.0 + 0.02 * n(ks[20], (D,)),
    }


def reference(x, norm1_g, w_in, pool_w, pool_scale, dw_w, dw_b, conv_ln_g, conv_ln_b, pw_w, pw_b,
              w_out, norm2_g, rg_w, rg_b, re_w, re_b, w_gate, w_up, w_down, final_g):
    B, S, _ = x.shape
    for l in range(DEPTH):
        h = rmsnorm(x, norm1_g[l])
        proj = h @ w_in[l]
        q, k, v, qi, ki, wi, pool_in, conv_in = split_columns(proj)
        q = q.reshape(B, S, ATT_HEADS, ATT_HEAD_DIM)
        k = k.reshape(B, S, ATT_HEADS, ATT_HEAD_DIM)
        v = v.reshape(B, S, ATT_HEADS, ATT_HEAD_DIM)
        qi = qi.reshape(B, S, IDX_HEADS, IDX_HEAD_DIM)
        y_a = dsa_attention(q, k, v, qi, ki, wi)
        y_b = pool_mixer(pool_in, pool_w[l], pool_scale[l])
        y_c = conformer_conv(conv_in, dw_w[l], dw_b[l], conv_ln_g[l], conv_ln_b[l], pw_w[l], pw_b[l])
        x = x + jnp.concatenate([y_a, y_b, y_c], axis=-1) @ w_out[l]
        h2 = rmsnorm(x, norm2_g[l])
        x = x + hier_moe(h2, rg_w[l], rg_b[l], re_w[l], re_b[l], w_gate[l], w_up[l], w_down[l])
    return rmsnorm(x, final_g)
```

```python
import functools

import jax
import jax.numpy as jnp
from jax import lax
from jax.experimental import pallas as pl
from jax.experimental.pallas import tpu as pltpu

F32 = jnp.float32
MXU_DTYPE = jnp.bfloat16

D_MODEL = 1024
CHUNK = 64
ATT_HEADS = 8
ATT_HEAD_DIM = 64
ATT_WIDTH = ATT_HEADS * ATT_HEAD_DIM
IDX_HEADS = 8
IDX_HEAD_DIM = 64
TOPK_MAX = 256
POOL_GROUPS = 4
POOL_GROUP_DIM = 64
POOL_WIDTH = POOL_GROUPS * POOL_GROUP_DIM
POOL_WINDOWS = (2, 4, 8, 16)
CONV_WIDTH = 256
CONV_KERNEL = 31
N_GROUPS = 4
EXPERTS_PER_GROUP = 4
N_EXPERTS = N_GROUPS * EXPERTS_PER_GROUP
EXPERT_HIDDEN = 512
EPS = 1e-6

LANES = 128
INT_MIN = -2 ** 31
NEG_INF = float("-inf")

N16 = ATT_WIDTH + 2 * IDX_HEAD_DIM
KI_COL_BLOCK = ATT_WIDTH // LANES
N32 = POOL_WIDTH + 2 * CONV_WIDTH
NT_ROWS = 3 * ATT_WIDTH
WI_ROWS = 16
KEY_NEG_INF = (0xFF800000 ^ 0x7FFFFFFF) - 2 ** 32

QB = 256
TK = 256
HALO = 32
VMEM_LIMIT = 48 * 1024 * 1024

_NT = (((1,), (1,)), ((), ()))


def _rms(x, g):
    return x * lax.rsqrt(jnp.mean(x * x, axis=-1, keepdims=True) + EPS) * g


def _proj_kernel(x_ref, g_ref, w_ref, wt_ref, wwit_ref, o16_ref, o32_ref, qt_ref, vt_ref, wit_ref, *, tm):
    h = _rms(x_ref[...], g_ref[...]).astype(MXU_DTYPE)
    p = jnp.dot(h, w_ref[...], preferred_element_type=F32)
    o16_ref[...] = p[:, :N16].astype(o16_ref.dtype)
    o32_ref[...] = p[:, N16:]
    pt = lax.dot_general(wt_ref[...], h, _NT, preferred_element_type=F32)
    qt_ref[...] = pt[:2 * ATT_WIDTH, :].astype(qt_ref.dtype)
    for c in range(tm // TK):
        vt_ref[c] = pt[2 * ATT_WIDTH:, c * TK:(c + 1) * TK].astype(vt_ref.dtype)
    wit_ref[...] = lax.dot_general(wwit_ref[...], h, _NT, preferred_element_type=F32)


def _proj(x2d, g, w, wt, wwit, tm):
    t = x2d.shape[0]
    return pl.pallas_call(
        functools.partial(_proj_kernel, tm=tm),
        grid=(t // tm,),
        in_specs=[
            pl.BlockSpec((tm, D_MODEL), lambda i: (i, 0)),
            pl.BlockSpec((1, D_MODEL), lambda i: (0, 0)),
            pl.BlockSpec((D_MODEL, N16 + N32), lambda i: (0, 0)),
            pl.BlockSpec((NT_ROWS, D_MODEL), lambda i: (0, 0)),
            pl.BlockSpec((WI_ROWS, D_MODEL), lambda i: (0, 0)),
        ],
        out_specs=[
            pl.BlockSpec((tm, N16), lambda i: (i, 0)),
            pl.BlockSpec((tm, N32), lambda i: (i, 0)),
            pl.BlockSpec((2 * ATT_WIDTH, tm), lambda i: (0, i)),
            pl.BlockSpec((tm // TK, ATT_WIDTH, TK), lambda i: (i, 0, 0)),
            pl.BlockSpec((WI_ROWS, tm), lambda i: (0, i)),
        ],
        out_shape=[
            jax.ShapeDtypeStruct((t, N16), MXU_DTYPE),
            jax.ShapeDtypeStruct((t, N32), F32),
            jax.ShapeDtypeStruct((2 * ATT_WIDTH, t), MXU_DTYPE),
            jax.ShapeDtypeStruct((t // TK, ATT_WIDTH, TK), MXU_DTYPE),
            jax.ShapeDtypeStruct((WI_ROWS, t), F32),
        ],
        compiler_params=pltpu.CompilerParams(
            dimension_semantics=("parallel",), vmem_limit_bytes=VMEM_LIMIT),
        name="proj",
    )(x2d, g, w, wt, wwit)


def _attn_kernel(qt_ref, qit_ref, k_ref, ki_ref, vt_ref, wit_ref, o_ref,
                 qm_ref, qim_ref, tri_ref, sc_ref, lg_ref, p_ref, acc_ref, m_ref, l_ref, *, ktop):
    j = pl.program_id(1)
    nkt = j + 1

    row = lax.broadcasted_iota(jnp.int32, (LANES, QB), 0)
    for h in range(ATT_HEADS):
        pr, half = divmod(h, 2)
        keep = (row < ATT_HEAD_DIM) if half == 0 else (row >= ATT_HEAD_DIM)
        qp = qt_ref[pr * LANES:(pr + 1) * LANES, :].astype(F32) * (ATT_HEAD_DIM ** -0.5)
        qm_ref[h] = jnp.where(keep, qp, 0.0).astype(qm_ref.dtype)
        qip = qit_ref[pr * LANES:(pr + 1) * LANES, :].astype(F32) * (IDX_HEAD_DIM ** -0.5)
        qim_ref[h] = jnp.where(keep, qip, 0.0).astype(qim_ref.dtype)

    r_i = lax.broadcasted_iota(jnp.int32, (TK, TK), 0)
    c_i = lax.broadcasted_iota(jnp.int32, (TK, TK), 1)
    tri_ref[...] = jnp.where(c_i <= r_i, 1.0, 0.0).astype(tri_ref.dtype)

    wt = wit_ref[...]
    q_chunk = (j * QB + lax.broadcasted_iota(jnp.int32, (1, QB), 1)) // CHUNK

    def admissible(k0):
        k_chunk = (k0 + lax.broadcasted_iota(jnp.int32, (TK, 1), 0)) // CHUNK
        return k_chunk <= q_chunk

    def score_tile(kt, carry):
        k0 = pl.multiple_of(kt * TK, TK)
        kit = ki_ref[pl.ds(k0, TK), :]
        acc = jnp.zeros((TK, QB), F32)
        for h in range(IDX_HEADS):
            d = jnp.dot(kit, qim_ref[h], preferred_element_type=F32)
            acc = acc + jnp.maximum(d, 0.0) * wt[h:h + 1, :]
        sc_ref[kt] = jnp.where(admissible(k0), acc * (IDX_HEADS ** -0.5), NEG_INF)
        return carry

    lax.fori_loop(0, nkt, score_tile, 0)

    def count(pred_fn):
        def body(kt, cnt):
            ones = jnp.where(pred_fn(sc_ref[kt]), 1.0, 0.0)
            return cnt + jnp.sum(ones.reshape(TK // 8, 8, QB), axis=0)
        cnt = lax.fori_loop(0, nkt, body, jnp.zeros((8, QB), F32))
        return jnp.sum(cnt, axis=0, keepdims=True)

    def as_float(u):
        key = u ^ INT_MIN
        bits = key ^ ((key >> 31) & 0x7FFFFFFF)
        return jnp.where(key < KEY_NEG_INF, NEG_INF, lax.bitcast_convert_type(bits, F32))

    def bit_step(i, prefix):
        cand_u = prefix | lax.shift_left(jnp.int32(1), 31 - i)
        cand = as_float(cand_u)
        total = count(lambda s: s >= cand)
        return jnp.where(total >= ktop, cand_u, prefix)

    prefix = lax.fori_loop(0, 32, bit_step, jnp.zeros((1, QB), jnp.int32))
    thr = as_float(prefix)
    need = ktop - count(lambda s: s > thr)

    m_ref[...] = jnp.full(m_ref.shape, NEG_INF, F32)
    l_ref[...] = jnp.zeros(l_ref.shape, F32)
    acc_ref[...] = jnp.zeros(acc_ref.shape, F32)

    def attend_tile(kt, eq_before):
        k0 = pl.multiple_of(kt * TK, TK)
        s = sc_ref[kt]
        eq = s == thr
        eqf = jnp.where(eq, 1.0, 0.0)
        incl = jnp.dot(tri_ref[...], eqf.astype(tri_ref.dtype), preferred_element_type=F32)
        sel = ((s > thr) | (eq & ((eq_before + incl) <= need))) & admissible(k0)
        bias = jnp.where(sel, 0.0, NEG_INF)
        alphas = []
        for h in range(ATT_HEADS):
            pr = h // 2
            kp = k_ref[pl.ds(k0, TK), pr * LANES:(pr + 1) * LANES]
            lg = jnp.dot(kp, qm_ref[h], preferred_element_type=F32) + bias
            lg_ref[h] = lg
            m_old = m_ref[h:h + 1, :]
            m_new = jnp.maximum(m_old, jnp.max(lg, axis=0, keepdims=True))
            m_ref[h:h + 1, :] = m_new
            m_safe = jnp.where(m_new == NEG_INF, 0.0, m_new)
            alphas.append((jnp.exp(m_old - m_safe), m_safe))
        for h in range(ATT_HEADS):
            alpha, m_safe = alphas[h]
            p = jnp.exp(lg_ref[h] - m_safe)
            l_ref[h:h + 1, :] = alpha * l_ref[h:h + 1, :] + jnp.sum(p, axis=0, keepdims=True)
            p_ref[h] = p.astype(p_ref.dtype)
        for h in range(ATT_HEADS):
            pr, half = divmod(h, 2)
            pv = jnp.dot(vt_ref[kt, pr * LANES:(pr + 1) * LANES, :], p_ref[h],
                         preferred_element_type=F32)
            rows = slice(h * ATT_HEAD_DIM, (h + 1) * ATT_HEAD_DIM)
            acc_ref[rows, :] = (alphas[h][0] * acc_ref[rows, :]
                                + pv[half * ATT_HEAD_DIM:(half + 1) * ATT_HEAD_DIM, :])
        return eq_before + jnp.sum(eqf, axis=0, keepdims=True)

    lax.fori_loop(0, nkt, attend_tile, jnp.zeros((1, QB), F32))

    for h in range(ATT_HEADS):
        rows = slice(h * ATT_HEAD_DIM, (h + 1) * ATT_HEAD_DIM)
        acc_ref[rows, :] = acc_ref[rows, :] / l_ref[h:h + 1, :]
    o_ref[...] = acc_ref[...].T.astype(o_ref.dtype)


def _attn(qt, o16, vt, wit, batch, seq):
    t = batch * seq
    nq = seq // QB
    nkt = seq // TK
    ktop = min(TOPK_MAX, seq // 4)
    return pl.pallas_call(
        functools.partial(_attn_kernel, ktop=ktop),
        grid=(batch, nq),
        in_specs=[
            pl.BlockSpec((ATT_WIDTH, QB), lambda b, j: (0, b * nq + j)),
            pl.BlockSpec((ATT_WIDTH, QB), lambda b, j: (1, b * nq + j)),
            pl.BlockSpec((seq, ATT_WIDTH), lambda b, j: (b, 0)),
            pl.BlockSpec((seq, LANES), lambda b, j: (b, KI_COL_BLOCK)),
            pl.BlockSpec((nkt, ATT_WIDTH, TK), lambda b, j: (b, 0, 0)),
            pl.BlockSpec((WI_ROWS, QB), lambda b, j: (0, b * nq + j)),
        ],
        out_specs=pl.BlockSpec((QB, ATT_WIDTH), lambda b, j: (b * nq + j, 0)),
        out_shape=jax.ShapeDtypeStruct((t, ATT_WIDTH), MXU_DTYPE),
        scratch_shapes=[
            pltpu.VMEM((ATT_HEADS, LANES, QB), MXU_DTYPE),
            pltpu.VMEM((IDX_HEADS, LANES, QB), MXU_DTYPE),
            pltpu.VMEM((TK, TK), MXU_DTYPE),
            pltpu.VMEM((nkt, TK, QB), F32),
            pltpu.VMEM((ATT_HEADS, TK, QB), F32),
            pltpu.VMEM((ATT_HEADS, TK, QB), MXU_DTYPE),
            pltpu.VMEM((ATT_WIDTH, QB), F32),
            pltpu.VMEM((ATT_HEADS, QB), F32),
            pltpu.VMEM((ATT_HEADS, QB), F32),
        ],
        compiler_params=pltpu.CompilerParams(
            dimension_semantics=("parallel", "parallel"), vmem_limit_bytes=VMEM_LIMIT),
        name="attn",
    )(qt, qt, o16, o16, vt, wit)


def _mix_kernel(x_ref, ya_ref, cur_ref, halo_ref, wp_ref, ps_ref, dw_ref, dwb_ref, lng_ref, lnb_ref,
                pw_ref, pwb_ref, wo_ref, o_ref, ubuf, hbuf, *, tm):
    j = pl.program_id(1)
    cur = cur_ref[...]
    halo = jnp.where(j > 0, halo_ref[...], 0.0)

    def glu(z):
        return z[:, POOL_WIDTH:POOL_WIDTH + CONV_WIDTH] * jax.nn.sigmoid(z[:, POOL_WIDTH + CONV_WIDTH:])

    u = cur[:, :POOL_WIDTH]
    ubuf[0:HALO, :] = halo[:, :POOL_WIDTH]
    ubuf[HALO:, :] = u
    hbuf[0:HALO, :] = glu(halo)
    hbuf[HALO:, :] = glu(cur)

    lane = lax.broadcasted_iota(jnp.int32, (tm, LANES), 1)
    upper = lane >= POOL_GROUP_DIM
    s0 = u[:, :LANES]
    s1 = u[:, LANES:]
    for i in range(1, POOL_WINDOWS[3]):
        if i < POOL_WINDOWS[1]:
            sh = ubuf[HALO - i:HALO - i + tm, 0:LANES]
            s0 = s0 + (sh if i < POOL_WINDOWS[0] else jnp.where(upper, sh, 0.0))
        sh = ubuf[HALO - i:HALO - i + tm, LANES:2 * LANES]
        s1 = s1 + (sh if i < POOL_WINDOWS[2] else jnp.where(upper, sh, 0.0))
    t1 = (j * tm + lax.broadcasted_iota(jnp.int32, (tm, LANES), 0) + 1).astype(F32)
    w0 = jnp.where(upper, float(POOL_WINDOWS[1]), float(POOL_WINDOWS[0]))
    w1 = jnp.where(upper, float(POOL_WINDOWS[3]), float(POOL_WINDOWS[2]))
    pooled = jnp.concatenate([s0 / jnp.minimum(t1, w0), s1 / jnp.minimum(t1, w1)], axis=1)
    d = (pooled - u).astype(MXU_DTYPE)
    yb = jnp.dot(d, wp_ref[...], preferred_element_type=F32) * ps_ref[...]

    c = jnp.zeros((tm, CONV_WIDTH), F32) + dwb_ref[...]
    off = HALO - (CONV_KERNEL - 1)
    for jj in range(CONV_KERNEL):
        c = c + hbuf[off + jj:off + jj + tm, :] * dw_ref[jj:jj + 1, :]
    mu = jnp.mean(c, axis=-1, keepdims=True)
    cc = c - mu
    var = jnp.mean(cc * cc, axis=-1, keepdims=True)
    hn = cc * lax.rsqrt(var + EPS) * lng_ref[...] + lnb_ref[...]
    sw = (hn * jax.nn.sigmoid(hn)).astype(MXU_DTYPE)
    yc = jnp.dot(sw, pw_ref[...], preferred_element_type=F32) + pwb_ref[...]

    y = jnp.dot(ya_ref[...], wo_ref[0:ATT_WIDTH, :], preferred_element_type=F32)
    y = y + jnp.dot(yb.astype(MXU_DTYPE), wo_ref[ATT_WIDTH:ATT_WIDTH + POOL_WIDTH, :], preferred_element_type=F32)
    y = y + jnp.dot(yc.astype(MXU_DTYPE), wo_ref[ATT_WIDTH + POOL_WIDTH:, :], preferred_element_type=F32)
    o_ref[...] = x_ref[...] + y


def _mix(x2d, ya, o32, wp, ps, dw, dwb, lng, lnb, pw, pwb, wo, batch, seq, tm):
    t = batch * seq
    nt = seq // tm
    hb = tm // HALO
    full = lambda b, j: (0, 0)
    return pl.pallas_call(
        functools.partial(_mix_kernel, tm=tm),
        grid=(batch, nt),
        in_specs=[
            pl.BlockSpec((tm, D_MODEL), lambda b, j: (b * nt + j, 0)),
            pl.BlockSpec((tm, ATT_WIDTH), lambda b, j: (b * nt + j, 0)),
            pl.BlockSpec((tm, N32), lambda b, j: (b * nt + j, 0)),
            pl.BlockSpec((HALO, N32), lambda b, j: (jnp.maximum((b * nt + j) * hb - 1, 0), 0)),
            pl.BlockSpec((POOL_WIDTH, POOL_WIDTH), full),
            pl.BlockSpec((1, POOL_WIDTH), full),
            pl.BlockSpec((HALO, CONV_WIDTH), full),
            pl.BlockSpec((1, CONV_WIDTH), full),
            pl.BlockSpec((1, CONV_WIDTH), full),
            pl.BlockSpec((1, CONV_WIDTH), full),
            pl.BlockSpec((CONV_WIDTH, CONV_WIDTH), full),
            pl.BlockSpec((1, CONV_WIDTH), full),
            pl.BlockSpec((D_MODEL, D_MODEL), full),
        ],
        out_specs=pl.BlockSpec((tm, D_MODEL), lambda b, j: (b * nt + j, 0)),
        out_shape=jax.ShapeDtypeStruct((t, D_MODEL), F32),
        scratch_shapes=[
            pltpu.VMEM((HALO + tm, POOL_WIDTH), F32),
            pltpu.VMEM((HALO + tm, CONV_WIDTH), F32),
        ],
        compiler_params=pltpu.CompilerParams(
            dimension_semantics=("parallel", "parallel"), vmem_limit_bytes=VMEM_LIMIT),
        name="mix",
    )(x2d, ya, o32, o32, wp, ps, dw, dwb, lng, lnb, pw, pwb, wo)


def _route(glog, elog):
    lane = lax.broadcasted_iota(jnp.int32, glog.shape, 1)
    lane_f = lane.astype(F32)
    big = float(LANES)
    gl = jnp.where(lane < N_GROUPS, glog, NEG_INF)
    ge = jnp.exp(gl - jnp.max(gl, axis=-1, keepdims=True))
    gp = ge / jnp.sum(ge, axis=-1, keepdims=True)
    p_g = jnp.max(gp, axis=-1, keepdims=True)
    g_sel = jnp.min(jnp.where(gp == p_g, lane_f, big), axis=-1, keepdims=True)
    in_grp = (lane // EXPERTS_PER_GROUP).astype(F32) == g_sel
    el = jnp.where(in_grp, elog, NEG_INF)
    ee = jnp.exp(el - jnp.max(el, axis=-1, keepdims=True))
    ep = ee / jnp.sum(ee, axis=-1, keepdims=True)
    ep = jnp.where(in_grp, ep, -1.0)
    v1 = jnp.max(ep, axis=-1, keepdims=True)
    i1 = jnp.min(jnp.where(ep == v1, lane_f, big), axis=-1, keepdims=True)
    ep2 = jnp.where(lane_f == i1, -1.0, ep)
    v2 = jnp.max(ep2, axis=-1, keepdims=True)
    i2 = jnp.min(jnp.where(ep2 == v2, lane_f, big), axis=-1, keepdims=True)
    den = v1 + v2
    w_e = jnp.where(lane_f == i1, v1 / den, jnp.where(lane_f == i2, v2 / den, 0.0))
    return p_g * w_e


def _moe_kernel(x_ref, g2_ref, rgw_ref, rgb_ref, rew_ref, reb_ref, wg_ref, wu_ref, wd_ref, fg_ref,
                o_ref, h_ref, gate_ref, acc_ref, *, final_norm):
    e = pl.program_id(1)

    @pl.when(e == 0)
    def _():
        x = x_ref[...]
        hb = _rms(x, g2_ref[...]).astype(MXU_DTYPE)
        h_ref[...] = hb
        glog = jnp.dot(hb, rgw_ref[...], preferred_element_type=F32) + rgb_ref[...]
        elog = jnp.dot(hb, rew_ref[...], preferred_element_type=F32) + reb_ref[...]
        gate_ref[...] = _route(glog, elog)
        acc_ref[...] = x

    hb = h_ref[...]
    a = jnp.dot(hb, wg_ref[0].astype(MXU_DTYPE), preferred_element_type=F32)
    b = jnp.dot(hb, wu_ref[0].astype(MXU_DTYPE), preferred_element_type=F32)
    lane = lax.broadcasted_iota(jnp.int32, gate_ref.shape, 1)
    g_col = jnp.sum(jnp.where(lane == e, gate_ref[...], 0.0), axis=-1, keepdims=True)
    act = (a * jax.nn.sigmoid(a)) * b * g_col
    acc_ref[...] += jnp.dot(act.astype(MXU_DTYPE), wd_ref[0].astype(MXU_DTYPE), preferred_element_type=F32)

    @pl.when(e == N_EXPERTS - 1)
    def _():
        out = acc_ref[...]
        if final_norm:
            out = _rms(out, fg_ref[...])
        o_ref[...] = out


def _moe(x2d, g2, rgw, rgb, rew, reb, wg, wu, wd, fg, layer, final_norm, tm):
    t = x2d.shape[0]
    full = lambda i, e: (0, 0)
    return pl.pallas_call(
        functools.partial(_moe_kernel, final_norm=final_norm),
        grid=(t // tm, N_EXPERTS),
        in_specs=[
            pl.BlockSpec((tm, D_MODEL), lambda i, e: (i, 0)),
            pl.BlockSpec((1, D_MODEL), full),
            pl.BlockSpec((D_MODEL, LANES), full),
            pl.BlockSpec((1, LANES), full),
            pl.BlockSpec((D_MODEL, LANES), full),
            pl.BlockSpec((1, LANES), full),
            pl.BlockSpec((1, D_MODEL, EXPERT_HIDDEN), lambda i, e: (layer * N_EXPERTS + e, 0, 0)),
            pl.BlockSpec((1, D_MODEL, EXPERT_HIDDEN), lambda i, e: (layer * N_EXPERTS + e, 0, 0)),
            pl.BlockSpec((1, EXPERT_HIDDEN, D_MODEL), lambda i, e: (layer * N_EXPERTS + e, 0, 0)),
            pl.BlockSpec((1, D_MODEL), full),
        ],
        out_specs=pl.BlockSpec((tm, D_MODEL), lambda i, e: (i, 0)),
        out_shape=jax.ShapeDtypeStruct((t, D_MODEL), F32),
        scratch_shapes=[
            pltpu.VMEM((tm, D_MODEL), MXU_DTYPE),
            pltpu.VMEM((tm, LANES), F32),
            pltpu.VMEM((tm, D_MODEL), F32),
        ],
        compiler_params=pltpu.CompilerParams(
            dimension_semantics=("parallel", "arbitrary"), vmem_limit_bytes=VMEM_LIMIT),
        name="moe",
    )(x2d, g2, rgw, rgb, rew, reb, wg, wu, wd, fg)


def _pad_lanes(w):
    return jnp.pad(w, ((0, 0), (0, LANES - w.shape[-1])))


def _block_diag(blocks):
    g, n, _ = blocks.shape
    out = jnp.zeros((g * n, g * n), blocks.dtype)
    for i in range(g):
        out = out.at[i * n:(i + 1) * n, i * n:(i + 1) * n].set(blocks[i])
    return out


def kernel(x, norm1_g, w_in, pool_w, pool_scale, dw_w, dw_b, conv_ln_g, conv_ln_b, pw_w, pw_b,
           w_out, norm2_g, rg_w, rg_b, re_w, re_b, w_gate, w_up, w_down, final_g):
    batch, seq, d = x.shape
    depth = w_in.shape[0]
    t = batch * seq
    tm = min(512, seq)
    tm_moe = min(1024, t)
    assert d == D_MODEL and seq % QB == 0 and seq % tm == 0 and tm % TK == 0 and t % tm_moe == 0

    o_q, o_k, o_v = 0, ATT_WIDTH, 2 * ATT_WIDTH
    o_qi = 3 * ATT_WIDTH
    o_ki = o_qi + IDX_HEADS * IDX_HEAD_DIM
    o_wi = o_ki + IDX_HEAD_DIM
    o_pool = o_wi + IDX_HEADS
    o_conv = o_pool + POOL_WIDTH

    wg = w_gate.reshape(depth * N_EXPERTS, D_MODEL, EXPERT_HIDDEN)
    wu = w_up.reshape(depth * N_EXPERTS, D_MODEL, EXPERT_HIDDEN)
    wd = w_down.reshape(depth * N_EXPERTS, EXPERT_HIDDEN, D_MODEL)
    fg = final_g.reshape(1, D_MODEL)

    xf = x.reshape(t, D_MODEL)
    for l in range(depth):
        w = w_in[l]
        w_ki = w[:, o_ki:o_wi]
        w_cat = jnp.concatenate([w[:, o_k:o_v], w_ki, w_ki, w[:, o_pool:]], axis=1).astype(MXU_DTYPE)
        wt = jnp.concatenate([w[:, o_q:o_k], w[:, o_qi:o_ki], w[:, o_v:o_qi]], axis=1).T.astype(MXU_DTYPE)
        wwit = jnp.pad(w[:, o_wi:o_pool].T, ((0, WI_ROWS - IDX_HEADS), (0, 0))).astype(MXU_DTYPE)
        o16, o32, qt, vt, wit = _proj(xf, norm1_g[l].reshape(1, D_MODEL), w_cat, wt, wwit, tm)

        ya = _attn(qt, o16, vt, wit, batch, seq)

        x1 = _mix(
            xf, ya, o32,
            _block_diag(pool_w[l]).astype(MXU_DTYPE), pool_scale[l].reshape(1, POOL_WIDTH),
            jnp.pad(dw_w[l], ((0, HALO - CONV_KERNEL), (0, 0))), dw_b[l].reshape(1, CONV_WIDTH),
            conv_ln_g[l].reshape(1, CONV_WIDTH), conv_ln_b[l].reshape(1, CONV_WIDTH),
            pw_w[l].astype(MXU_DTYPE), pw_b[l].reshape(1, CONV_WIDTH),
            w_out[l].astype(MXU_DTYPE), batch, seq, tm)

        xf = _moe(
            x1, norm2_g[l].reshape(1, D_MODEL),
            _pad_lanes(rg_w[l]).astype(MXU_DTYPE), _pad_lanes(rg_b[l].reshape(1, N_GROUPS)),
            _pad_lanes(re_w[l]).astype(MXU_DTYPE), _pad_lanes(re_b[l].reshape(1, N_EXPERTS)),
            wg, wu, wd, fg, l, l == depth - 1, tm_moe)
    return xf.reshape(batch, seq, D_MODEL)
```

```python
import functools

import jax
import jax.numpy as jnp
from jax import lax
from jax.experimental import pallas as pl
from jax.experimental.pallas import tpu as pltpu

F32 = jnp.float32
MXU_DTYPE = jnp.bfloat16

D_MODEL = 1024
CHUNK = 64
ATT_HEADS = 8
ATT_HEAD_DIM = 64
ATT_WIDTH = ATT_HEADS * ATT_HEAD_DIM
IDX_HEADS = 8
IDX_HEAD_DIM = 64
TOPK_MAX = 256
POOL_GROUPS = 4
POOL_GROUP_DIM = 64
POOL_WIDTH = POOL_GROUPS * POOL_GROUP_DIM
POOL_WINDOWS = (2, 4, 8, 16)
CONV_WIDTH = 256
CONV_KERNEL = 31
N_GROUPS = 4
EXPERTS_PER_GROUP = 4
N_EXPERTS = N_GROUPS * EXPERTS_PER_GROUP
EXPERT_HIDDEN = 512
EPS = 1e-6

LANES = 128
INT_MIN = -2 ** 31
NEG_INF = float("-inf")

N16 = ATT_WIDTH + 2 * IDX_HEAD_DIM
KI_COL_BLOCK = ATT_WIDTH // LANES
N32 = POOL_WIDTH + 2 * CONV_WIDTH
NT_ROWS = 3 * ATT_WIDTH
WI_ROWS = 16
VT_ROWS = ATT_HEAD_DIM + 16
KEY_NEG_INF = (0xFF800000 ^ 0x7FFFFFFF) - 2 ** 32

QB = 256
TK = 256
HALO = 32
VMEM_LIMIT = 48 * 1024 * 1024
MOE_VMEM_LIMIT = 56 * 1024 * 1024
RB = 256

_NT = (((1,), (1,)), ((), ()))


def _rms(x, g):
    return x * lax.rsqrt(jnp.mean(x * x, axis=-1, keepdims=True) + EPS) * g


def _proj_kernel(x_ref, g_ref, w_ref, wt_ref, wwit_ref, o16_ref, o32_ref, qt_ref, vt_ref, wit_ref, *, tm):
    h = _rms(x_ref[...], g_ref[...]).astype(MXU_DTYPE)
    p = jnp.dot(h, w_ref[...], preferred_element_type=F32)
    o16_ref[...] = p[:, :N16].astype(o16_ref.dtype)
    o32_ref[...] = p[:, N16:]
    pt = lax.dot_general(wt_ref[...], h, _NT, preferred_element_type=F32)
    qt_ref[...] = pt[:2 * ATT_WIDTH, :].astype(qt_ref.dtype)
    ones = jnp.ones((VT_ROWS - ATT_HEAD_DIM, TK), vt_ref.dtype)
    for c in range(tm // TK):
        for hd in range(ATT_HEADS):
            r0 = 2 * ATT_WIDTH + hd * ATT_HEAD_DIM
            vt_ref[c, hd * VT_ROWS:hd * VT_ROWS + ATT_HEAD_DIM, :] = (
                pt[r0:r0 + ATT_HEAD_DIM, c * TK:(c + 1) * TK].astype(vt_ref.dtype))
            vt_ref[c, hd * VT_ROWS + ATT_HEAD_DIM:(hd + 1) * VT_ROWS, :] = ones
    wit_ref[...] = lax.dot_general(wwit_ref[...], h, _NT, preferred_element_type=F32)


def _proj(x2d, g, w, wt, wwit, tm):
    t = x2d.shape[0]
    return pl.pallas_call(
        functools.partial(_proj_kernel, tm=tm),
        grid=(t // tm,),
        in_specs=[
            pl.BlockSpec((tm, D_MODEL), lambda i: (i, 0)),
            pl.BlockSpec((1, D_MODEL), lambda i: (0, 0)),
            pl.BlockSpec((D_MODEL, N16 + N32), lambda i: (0, 0)),
            pl.BlockSpec((NT_ROWS, D_MODEL), lambda i: (0, 0)),
            pl.BlockSpec((WI_ROWS, D_MODEL), lambda i: (0, 0)),
        ],
        out_specs=[
            pl.BlockSpec((tm, N16), lambda i: (i, 0)),
            pl.BlockSpec((tm, N32), lambda i: (i, 0)),
            pl.BlockSpec((2 * ATT_WIDTH, tm), lambda i: (0, i)),
            pl.BlockSpec((tm // TK, ATT_HEADS * VT_ROWS, TK), lambda i: (i, 0, 0)),
            pl.BlockSpec((WI_ROWS, tm), lambda i: (0, i)),
        ],
        out_shape=[
            jax.ShapeDtypeStruct((t, N16), MXU_DTYPE),
            jax.ShapeDtypeStruct((t, N32), F32),
            jax.ShapeDtypeStruct((2 * ATT_WIDTH, t), MXU_DTYPE),
            jax.ShapeDtypeStruct((t // TK, ATT_HEADS * VT_ROWS, TK), MXU_DTYPE),
            jax.ShapeDtypeStruct((WI_ROWS, t), F32),
        ],
        compiler_params=pltpu.CompilerParams(
            dimension_semantics=("parallel",), vmem_limit_bytes=VMEM_LIMIT),
        name="proj",
    )(x2d, g, w, wt, wwit)


def _attn_kernel(qt_ref, qit_ref, k_ref, ki_ref, vt_ref, wit_ref, o_ref,
                 qm_ref, qim_ref, tri_ref, sc_ref, lg_ref, p_ref, acc_ref, m_ref, l_ref, *, ktop):
    j = pl.program_id(1)
    nkt = j + 1

    row = lax.broadcasted_iota(jnp.int32, (LANES, QB), 0)
    for h in range(ATT_HEADS):
        pr, half = divmod(h, 2)
        keep = (row < ATT_HEAD_DIM) if half == 0 else (row >= ATT_HEAD_DIM)
        qp = qt_ref[pr * LANES:(pr + 1) * LANES, :].astype(F32) * (ATT_HEAD_DIM ** -0.5)
        qm_ref[h] = jnp.where(keep, qp, 0.0).astype(qm_ref.dtype)
        qip = qit_ref[pr * LANES:(pr + 1) * LANES, :].astype(F32) * (IDX_HEAD_DIM ** -0.5)
        qim_ref[h] = jnp.where(keep, qip, 0.0).astype(qim_ref.dtype)

    r_i = lax.broadcasted_iota(jnp.int32, (TK, TK), 0)
    c_i = lax.broadcasted_iota(jnp.int32, (TK, TK), 1)
    tri_ref[...] = jnp.where(c_i <= r_i, 1.0, 0.0).astype(tri_ref.dtype)

    wt = wit_ref[...]
    q_chunk = (j * QB + lax.broadcasted_iota(jnp.int32, (1, QB), 1)) // CHUNK

    def admissible(k0):
        k_chunk = (k0 + lax.broadcasted_iota(jnp.int32, (TK, 1), 0)) // CHUNK
        return k_chunk <= q_chunk

    def score_tile(kt, carry):
        k0 = pl.multiple_of(kt * TK, TK)
        kit = ki_ref[pl.ds(k0, TK), :]
        acc = jnp.zeros((TK, QB), F32)
        for h in range(IDX_HEADS):
            d = jnp.dot(kit, qim_ref[h], preferred_element_type=F32)
            acc = acc + jnp.maximum(d, 0.0) * wt[h:h + 1, :]
        sc_ref[kt] = jnp.where(admissible(k0), acc * (IDX_HEADS ** -0.5), NEG_INF)
        return carry

    lax.fori_loop(0, nkt, score_tile, 0)

    def count(pred_fn):
        def body(kt, cnt):
            ones = jnp.where(pred_fn(sc_ref[kt]), 1.0, 0.0)
            return cnt + jnp.sum(ones.reshape(TK // 8, 8, QB), axis=0)
        cnt = lax.fori_loop(0, nkt, body, jnp.zeros((8, QB), F32))
        return jnp.sum(cnt, axis=0, keepdims=True)

    def as_float(u):
        key = u ^ INT_MIN
        bits = key ^ ((key >> 31) & 0x7FFFFFFF)
        return jnp.where(key < KEY_NEG_INF, NEG_INF, lax.bitcast_convert_type(bits, F32))

    def bit_step(i, prefix):
        cand_u = prefix | lax.shift_left(jnp.int32(1), 31 - i)
        cand = as_float(cand_u)
        total = count(lambda s: s >= cand)
        return jnp.where(total >= ktop, cand_u, prefix)

    prefix = lax.fori_loop(0, 32, bit_step, jnp.zeros((1, QB), jnp.int32))
    thr = as_float(prefix)
    need = ktop - count(lambda s: s > thr)

    m_ref[...] = jnp.full(m_ref.shape, NEG_INF, F32)
    l_ref[...] = jnp.zeros(l_ref.shape, F32)
    acc_ref[...] = jnp.zeros(acc_ref.shape, F32)

    def attend_tile(kt, eq_before):
        k0 = pl.multiple_of(kt * TK, TK)
        s = sc_ref[kt]
        eq = s == thr
        eqf = jnp.where(eq, 1.0, 0.0)
        incl = jnp.dot(tri_ref[...], eqf.astype(tri_ref.dtype), preferred_element_type=F32)
        sel = ((s > thr) | (eq & ((eq_before + incl) <= need))) & admissible(k0)
        bias = jnp.where(sel, 0.0, NEG_INF)
        alphas = []
        for h in range(ATT_HEADS):
            pr = h // 2
            kp = k_ref[pl.ds(k0, TK), pr * LANES:(pr + 1) * LANES]
            lg = jnp.dot(kp, qm_ref[h], preferred_element_type=F32) + bias
            lg_ref[h] = lg
            m_old = m_ref[h:h + 1, :]
            m_new = jnp.maximum(m_old, jnp.max(lg, axis=0, keepdims=True))
            m_ref[h:h + 1, :] = m_new
            m_safe = jnp.where(m_new == NEG_INF, 0.0, m_new)
            alphas.append((jnp.exp(m_old - m_safe), m_safe))
        for h in range(ATT_HEADS):
            alpha, m_safe = alphas[h]
            p_ref[h] = jnp.exp(lg_ref[h] - m_safe).astype(p_ref.dtype)
        for h in range(ATT_HEADS):
            alpha = alphas[h][0]
            pv = jnp.dot(vt_ref[kt, h * VT_ROWS:(h + 1) * VT_ROWS, :], p_ref[h],
                         preferred_element_type=F32)
            rows = slice(h * ATT_HEAD_DIM, (h + 1) * ATT_HEAD_DIM)
            acc_ref[rows, :] = alpha * acc_ref[rows, :] + pv[:ATT_HEAD_DIM, :]
            l_ref[h:h + 1, :] = alpha * l_ref[h:h + 1, :] + pv[ATT_HEAD_DIM:ATT_HEAD_DIM + 1, :]
        return eq_before + jnp.sum(eqf, axis=0, keepdims=True)

    lax.fori_loop(0, nkt, attend_tile, jnp.zeros((1, QB), F32))

    for h in range(ATT_HEADS):
        rows = slice(h * ATT_HEAD_DIM, (h + 1) * ATT_HEAD_DIM)
        acc_ref[rows, :] = acc_ref[rows, :] / l_ref[h:h + 1, :]
    o_ref[...] = acc_ref[...].T.astype(o_ref.dtype)


def _attn(qt, o16, vt, wit, batch, seq):
    t = batch * seq
    nq = seq // QB
    nkt = seq // TK
    ktop = min(TOPK_MAX, seq // 4)
    return pl.pallas_call(
        functools.partial(_attn_kernel, ktop=ktop),
        grid=(batch, nq),
        in_specs=[
            pl.BlockSpec((ATT_WIDTH, QB), lambda b, j: (0, b * nq + j)),
            pl.BlockSpec((ATT_WIDTH, QB), lambda b, j: (1, b * nq + j)),
            pl.BlockSpec((seq, ATT_WIDTH), lambda b, j: (b, 0)),
            pl.BlockSpec((seq, LANES), lambda b, j: (b, KI_COL_BLOCK)),
            pl.BlockSpec((nkt, ATT_HEADS * VT_ROWS, TK), lambda b, j: (b, 0, 0)),
            pl.BlockSpec((WI_ROWS, QB), lambda b, j: (0, b * nq + j)),
        ],
        out_specs=pl.BlockSpec((QB, ATT_WIDTH), lambda b, j: (b * nq + j, 0)),
        out_shape=jax.ShapeDtypeStruct((t, ATT_WIDTH), MXU_DTYPE),
        scratch_shapes=[
            pltpu.VMEM((ATT_HEADS, LANES, QB), MXU_DTYPE),
            pltpu.VMEM((IDX_HEADS, LANES, QB), MXU_DTYPE),
            pltpu.VMEM((TK, TK), MXU_DTYPE),
            pltpu.VMEM((nkt, TK, QB), F32),
            pltpu.VMEM((ATT_HEADS, TK, QB), F32),
            pltpu.VMEM((ATT_HEADS, TK, QB), MXU_DTYPE),
            pltpu.VMEM((ATT_WIDTH, QB), F32),
            pltpu.VMEM((ATT_HEADS, QB), F32),
            pltpu.VMEM((ATT_HEADS, QB), F32),
        ],
        compiler_params=pltpu.CompilerParams(
            dimension_semantics=("parallel", "parallel"), vmem_limit_bytes=VMEM_LIMIT),
        name="attn",
    )(qt, qt, o16, o16, vt, wit)


def _mix_kernel(x_ref, ya_ref, cur_ref, halo_ref, wp_ref, ps_ref, dw_ref, dwb_ref, lng_ref, lnb_ref,
                pw_ref, pwb_ref, wo_ref, o_ref, ubuf, hbuf, *, tm):
    j = pl.program_id(1)
    cur = cur_ref[...]
    halo = jnp.where(j > 0, halo_ref[...], 0.0)

    def glu(z):
        return z[:, POOL_WIDTH:POOL_WIDTH + CONV_WIDTH] * jax.nn.sigmoid(z[:, POOL_WIDTH + CONV_WIDTH:])

    u = cur[:, :POOL_WIDTH]
    ubuf[0:HALO, :] = halo[:, :POOL_WIDTH]
    ubuf[HALO:, :] = u
    hbuf[0:HALO, :] = glu(halo)
    hbuf[HALO:, :] = glu(cur)

    lane = lax.broadcasted_iota(jnp.int32, (tm, LANES), 1)
    upper = lane >= POOL_GROUP_DIM
    s0 = u[:, :LANES]
    s1 = u[:, LANES:]
    for i in range(1, POOL_WINDOWS[3]):
        if i < POOL_WINDOWS[1]:
            sh = ubuf[HALO - i:HALO - i + tm, 0:LANES]
            s0 = s0 + (sh if i < POOL_WINDOWS[0] else jnp.where(upper, sh, 0.0))
        sh = ubuf[HALO - i:HALO - i + tm, LANES:2 * LANES]
        s1 = s1 + (sh if i < POOL_WINDOWS[2] else jnp.where(upper, sh, 0.0))
    t1 = (j * tm + lax.broadcasted_iota(jnp.int32, (tm, LANES), 0) + 1).astype(F32)
    w0 = jnp.where(upper, float(POOL_WINDOWS[1]), float(POOL_WINDOWS[0]))
    w1 = jnp.where(upper, float(POOL_WINDOWS[3]), float(POOL_WINDOWS[2]))
    pooled = jnp.concatenate([s0 / jnp.minimum(t1, w0), s1 / jnp.minimum(t1, w1)], axis=1)
    d = (pooled - u).astype(MXU_DTYPE)
    yb = jnp.dot(d, wp_ref[...], preferred_element_type=F32) * ps_ref[...]

    c = jnp.zeros((tm, CONV_WIDTH), F32) + dwb_ref[...]
    off = HALO - (CONV_KERNEL - 1)
    for jj in range(CONV_KERNEL):
        c = c + hbuf[off + jj:off + jj + tm, :] * dw_ref[jj:jj + 1, :]
    mu = jnp.mean(c, axis=-1, keepdims=True)
    cc = c - mu
    var = jnp.mean(cc * cc, axis=-1, keepdims=True)
    hn = cc * lax.rsqrt(var + EPS) * lng_ref[...] + lnb_ref[...]
    sw = (hn * jax.nn.sigmoid(hn)).astype(MXU_DTYPE)
    yc = jnp.dot(sw, pw_ref[...], preferred_element_type=F32) + pwb_ref[...]

    y = jnp.dot(ya_ref[...], wo_ref[0:ATT_WIDTH, :], preferred_element_type=F32)
    y = y + jnp.dot(yb.astype(MXU_DTYPE), wo_ref[ATT_WIDTH:ATT_WIDTH + POOL_WIDTH, :], preferred_element_type=F32)
    y = y + jnp.dot(yc.astype(MXU_DTYPE), wo_ref[ATT_WIDTH + POOL_WIDTH:, :], preferred_element_type=F32)
    o_ref[...] = x_ref[...] + y


def _mix(x2d, ya, o32, wp, ps, dw, dwb, lng, lnb, pw, pwb, wo, batch, seq, tm):
    t = batch * seq
    nt = seq // tm
    hb = tm // HALO
    full = lambda b, j: (0, 0)
    return pl.pallas_call(
        functools.partial(_mix_kernel, tm=tm),
        grid=(batch, nt),
        in_specs=[
            pl.BlockSpec((tm, D_MODEL), lambda b, j: (b * nt + j, 0)),
            pl.BlockSpec((tm, ATT_WIDTH), lambda b, j: (b * nt + j, 0)),
            pl.BlockSpec((tm, N32), lambda b, j: (b * nt + j, 0)),
            pl.BlockSpec((HALO, N32), lambda b, j: (jnp.maximum((b * nt + j) * hb - 1, 0), 0)),
            pl.BlockSpec((POOL_WIDTH, POOL_WIDTH), full),
            pl.BlockSpec((1, POOL_WIDTH), full),
            pl.BlockSpec((HALO, CONV_WIDTH), full),
            pl.BlockSpec((1, CONV_WIDTH), full),
            pl.BlockSpec((1, CONV_WIDTH), full),
            pl.BlockSpec((1, CONV_WIDTH), full),
            pl.BlockSpec((CONV_WIDTH, CONV_WIDTH), full),
            pl.BlockSpec((1, CONV_WIDTH), full),
            pl.BlockSpec((D_MODEL, D_MODEL), full),
        ],
        out_specs=pl.BlockSpec((tm, D_MODEL), lambda b, j: (b * nt + j, 0)),
        out_shape=jax.ShapeDtypeStruct((t, D_MODEL), F32),
        scratch_shapes=[
            pltpu.VMEM((HALO + tm, POOL_WIDTH), F32),
            pltpu.VMEM((HALO + tm, CONV_WIDTH), F32),
        ],
        compiler_params=pltpu.CompilerParams(
            dimension_semantics=("parallel", "parallel"), vmem_limit_bytes=VMEM_LIMIT),
        name="mix",
    )(x2d, ya, o32, o32, wp, ps, dw, dwb, lng, lnb, pw, pwb, wo)


def _route(glog, elog):
    lane = lax.broadcasted_iota(jnp.int32, glog.shape, 1)
    lane_f = lane.astype(F32)
    big = float(LANES)
    gl = jnp.where(lane < N_GROUPS, glog, NEG_INF)
    ge = jnp.exp(gl - jnp.max(gl, axis=-1, keepdims=True))
    gp = ge / jnp.sum(ge, axis=-1, keepdims=True)
    p_g = jnp.max(gp, axis=-1, keepdims=True)
    g_sel = jnp.min(jnp.where(gp == p_g, lane_f, big), axis=-1, keepdims=True)
    in_grp = (lane // EXPERTS_PER_GROUP).astype(F32) == g_sel
    el = jnp.where(in_grp, elog, NEG_INF)
    ee = jnp.exp(el - jnp.max(el, axis=-1, keepdims=True))
    ep = ee / jnp.sum(ee, axis=-1, keepdims=True)
    ep = jnp.where(in_grp, ep, -1.0)
    v1 = jnp.max(ep, axis=-1, keepdims=True)
    i1 = jnp.min(jnp.where(ep == v1, lane_f, big), axis=-1, keepdims=True)
    ep2 = jnp.where(lane_f == i1, -1.0, ep)
    v2 = jnp.max(ep2, axis=-1, keepdims=True)
    i2 = jnp.min(jnp.where(ep2 == v2, lane_f, big), axis=-1, keepdims=True)
    den = v1 + v2
    w_e = jnp.where(lane_f == i1, v1 / den, jnp.where(lane_f == i2, v2 / den, 0.0))
    return p_g * w_e, g_sel


def _moe_kernel(x_ref, g2_ref, rgw_ref, rgb_ref, rew_ref, reb_ref, tri_ref, wgt_ref, wut_ref, wdt_ref, fg_ref,
                o_ref, ht_ref, gate3t_ref, info_ref, infot_ref, xgt_ref, ggt_ref, ygt_ref, acct_ref, cnt_ref,
                *, tm, final_norm):
    e = pl.program_id(1)
    grp = e // EXPERTS_PER_GROUP
    le = e % EXPERTS_PER_GROUP
    grp_f = grp.astype(F32)

    @pl.when(e == 0)
    def _():
        h = _rms(x_ref[...], g2_ref[...])
        hb = h.astype(MXU_DTYPE)
        ht_ref[...] = h.T.astype(MXU_DTYPE)
        glog = jnp.dot(hb, rgw_ref[...], preferred_element_type=F32) + rgb_ref[...]
        elog = jnp.dot(hb, rew_ref[...], preferred_element_type=F32) + reb_ref[...]
        gate, g_sel = _route(glog, elog)
        g1 = gate.astype(MXU_DTYPE).astype(F32)
        r1 = gate - g1
        g2 = r1.astype(MXU_DTYPE).astype(F32)
        gate3t_ref[0] = g1.T[:N_EXPERTS, :].astype(MXU_DTYPE)
        gate3t_ref[1] = g2.T[:N_EXPERTS, :].astype(MXU_DTYPE)
        gate3t_ref[2] = (r1 - g2).T[:N_EXPERTS, :].astype(MXU_DTYPE)
        lane = lax.broadcasted_iota(jnp.int32, (tm, LANES), 1)
        member = jnp.where(lane.astype(F32) == g_sel, 1.0, 0.0).astype(MXU_DTYPE)
        ranks = jnp.dot(tri_ref[...], member, preferred_element_type=F32)
        info = jnp.where(lane < N_GROUPS, ranks, jnp.where(lane == N_GROUPS, g_sel, 0.0))
        info_ref[...] = info
        infot_ref[...] = info.T[:8, :]
        for gi in range(N_GROUPS):
            cnt_ref[gi] = ranks[tm - 1, gi].astype(jnp.int32)
        acct_ref[...] = jnp.zeros(acct_ref.shape, F32)

    n_rows = cnt_ref[grp]
    n_blk = (n_rows + (RB - 1)) // RB

    @pl.when(le == 0)
    def _():
        lane = lax.broadcasted_iota(jnp.int32, (tm, LANES), 1)
        info = info_ref[...]
        rank_c = jnp.sum(jnp.where(lane == grp, info, 0.0), axis=-1, keepdims=True)
        member_c = jnp.sum(jnp.where(lane == N_GROUPS, info, 0.0), axis=-1, keepdims=True) == grp_f

        def gather(rb, carry):
            want = (rb * RB + 1 + lax.broadcasted_iota(jnp.int32, (tm, RB), 1)).astype(F32)
            pick = jnp.where(member_c & (rank_c == want), 1.0, 0.0).astype(MXU_DTYPE)
            xgt_ref[rb] = jnp.dot(ht_ref[...], pick, preferred_element_type=F32).astype(xgt_ref.dtype)
            ggt_ref[rb] = (jnp.dot(gate3t_ref[0], pick, preferred_element_type=F32)
                           + jnp.dot(gate3t_ref[1], pick, preferred_element_type=F32)
                           + jnp.dot(gate3t_ref[2], pick, preferred_element_type=F32))
            ygt_ref[rb] = jnp.zeros((D_MODEL, RB), F32)
            return carry

        lax.fori_loop(0, n_blk, gather, 0)

    def expert(rb, carry):
        xb = xgt_ref[rb]
        a = jnp.dot(wgt_ref[0], xb, preferred_element_type=F32)
        b = jnp.dot(wut_ref[0], xb, preferred_element_type=F32)
        act = (a * jax.nn.sigmoid(a)) * b * ggt_ref[rb, pl.ds(e, 1), :]
        ygt_ref[rb] += jnp.dot(wdt_ref[0], act.astype(MXU_DTYPE), preferred_element_type=F32)
        return carry

    lax.fori_loop(0, n_blk, expert, 0)

    @pl.when(le == EXPERTS_PER_GROUP - 1)
    def _():
        rank_t = infot_ref[pl.ds(grp, 1), :]
        member_t = infot_ref[N_GROUPS:N_GROUPS + 1, :] == grp_f

        def scatter(rb, carry):
            want = (rb * RB + 1 + lax.broadcasted_iota(jnp.int32, (RB, tm), 0)).astype(F32)
            put = jnp.where(member_t & (rank_t == want), 1.0, 0.0).astype(MXU_DTYPE)
            y = ygt_ref[rb]
            y_hi = y.astype(MXU_DTYPE)
            y_lo = (y - y_hi.astype(F32)).astype(MXU_DTYPE)
            acct_ref[...] += (jnp.dot(y_hi, put, preferred_element_type=F32)
                              + jnp.dot(y_lo, put, preferred_element_type=F32))
            return carry

        lax.fori_loop(0, n_blk, scatter, 0)

    @pl.when(e == N_EXPERTS - 1)
    def _():
        out = x_ref[...] + acct_ref[...].T
        if final_norm:
            out = _rms(out, fg_ref[...])
        o_ref[...] = out


def _moe(x2d, g2, rgw, rgb, rew, reb, tri, wgt, wut, wdt, fg, layer, final_norm, tm):
    t = x2d.shape[0]
    full = lambda i, e: (0, 0)
    return pl.pallas_call(
        functools.partial(_moe_kernel, tm=tm, final_norm=final_norm),
        grid=(t // tm, N_EXPERTS),
        in_specs=[
            pl.BlockSpec((tm, D_MODEL), lambda i, e: (i, 0)),
            pl.BlockSpec((1, D_MODEL), full),
            pl.BlockSpec((D_MODEL, LANES), full),
            pl.BlockSpec((1, LANES), full),
            pl.BlockSpec((D_MODEL, LANES), full),
            pl.BlockSpec((1, LANES), full),
            pl.BlockSpec((tm, tm), full),
            pl.BlockSpec((1, EXPERT_HIDDEN, D_MODEL), lambda i, e: (layer * N_EXPERTS + e, 0, 0)),
            pl.BlockSpec((1, EXPERT_HIDDEN, D_MODEL), lambda i, e: (layer * N_EXPERTS + e, 0, 0)),
            pl.BlockSpec((1, D_MODEL, EXPERT_HIDDEN), lambda i, e: (layer * N_EXPERTS + e, 0, 0)),
            pl.BlockSpec((1, D_MODEL), full),
        ],
        out_specs=pl.BlockSpec((tm, D_MODEL), lambda i, e: (i, 0)),
        out_shape=jax.ShapeDtypeStruct((t, D_MODEL), F32),
        scratch_shapes=[
            pltpu.VMEM((D_MODEL, tm), MXU_DTYPE),
            pltpu.VMEM((3, N_EXPERTS, tm), MXU_DTYPE),
            pltpu.VMEM((tm, LANES), F32),
            pltpu.VMEM((8, tm), F32),
            pltpu.VMEM((tm // RB, D_MODEL, RB), MXU_DTYPE),
            pltpu.VMEM((tm // RB, N_EXPERTS, RB), F32),
            pltpu.VMEM((tm // RB, D_MODEL, RB), F32),
            pltpu.VMEM((D_MODEL, tm), F32),
            pltpu.SMEM((N_GROUPS,), jnp.int32),
        ],
        compiler_params=pltpu.CompilerParams(
            dimension_semantics=("parallel", "arbitrary"), vmem_limit_bytes=MOE_VMEM_LIMIT),
        name="moe",
    )(x2d, g2, rgw, rgb, rew, reb, tri, wgt, wut, wdt, fg)


def _pad_lanes(w):
    return jnp.pad(w, ((0, 0), (0, LANES - w.shape[-1])))


def _block_diag(blocks):
    g, n, _ = blocks.shape
    out = jnp.zeros((g * n, g * n), blocks.dtype)
    for i in range(g):
        out = out.at[i * n:(i + 1) * n, i * n:(i + 1) * n].set(blocks[i])
    return out


def kernel(x, norm1_g, w_in, pool_w, pool_scale, dw_w, dw_b, conv_ln_g, conv_ln_b, pw_w, pw_b,
           w_out, norm2_g, rg_w, rg_b, re_w, re_b, w_gate, w_up, w_down, final_g):
    batch, seq, d = x.shape
    depth = w_in.shape[0]
    t = batch * seq
    tm = min(512, seq)
    tm_moe = min(1024, t)
    assert d == D_MODEL and seq % QB == 0 and seq % tm == 0 and tm % TK == 0 and t % tm_moe == 0

    o_q, o_k, o_v = 0, ATT_WIDTH, 2 * ATT_WIDTH
    o_qi = 3 * ATT_WIDTH
    o_ki = o_qi + IDX_HEADS * IDX_HEAD_DIM
    o_wi = o_ki + IDX_HEAD_DIM
    o_pool = o_wi + IDX_HEADS
    o_conv = o_pool + POOL_WIDTH

    wgt = jnp.swapaxes(w_gate, -1, -2).reshape(depth * N_EXPERTS, EXPERT_HIDDEN, D_MODEL).astype(MXU_DTYPE)
    wut = jnp.swapaxes(w_up, -1, -2).reshape(depth * N_EXPERTS, EXPERT_HIDDEN, D_MODEL).astype(MXU_DTYPE)
    wdt = jnp.swapaxes(w_down, -1, -2).reshape(depth * N_EXPERTS, D_MODEL, EXPERT_HIDDEN).astype(MXU_DTYPE)
    fg = final_g.reshape(1, D_MODEL)
    tri = jnp.tri(tm_moe, dtype=MXU_DTYPE)

    xf = x.reshape(t, D_MODEL)
    for l in range(depth):
        w = w_in[l]
        w_ki = w[:, o_ki:o_wi]
        w_cat = jnp.concatenate([w[:, o_k:o_v], w_ki, w_ki, w[:, o_pool:]], axis=1).astype(MXU_DTYPE)
        wt = jnp.concatenate([w[:, o_q:o_k], w[:, o_qi:o_ki], w[:, o_v:o_qi]], axis=1).T.astype(MXU_DTYPE)
        wwit = jnp.pad(w[:, o_wi:o_pool].T, ((0, WI_ROWS - IDX_HEADS), (0, 0))).astype(MXU_DTYPE)
        o16, o32, qt, vt, wit = _proj(xf, norm1_g[l].reshape(1, D_MODEL), w_cat, wt, wwit, tm)

        ya = _attn(qt, o16, vt, wit, batch, seq)

        x1 = _mix(
            xf, ya, o32,
            _block_diag(pool_w[l]).astype(MXU_DTYPE), pool_scale[l].reshape(1, POOL_WIDTH),
            jnp.pad(dw_w[l], ((0, HALO - CONV_KERNEL), (0, 0))), dw_b[l].reshape(1, CONV_WIDTH),
            conv_ln_g[l].reshape(1, CONV_WIDTH), conv_ln_b[l].reshape(1, CONV_WIDTH),
            pw_w[l].astype(MXU_DTYPE), pw_b[l].reshape(1, CONV_WIDTH),
            w_out[l].astype(MXU_DTYPE), batch, seq, tm)

        xf = _moe(
            x1, norm2_g[l].reshape(1, D_MODEL),
            _pad_lanes(rg_w[l]).astype(MXU_DTYPE), _pad_lanes(rg_b[l].reshape(1, N_GROUPS)),
            _pad_lanes(re_w[l]).astype(MXU_DTYPE), _pad_lanes(re_b[l].reshape(1, N_EXPERTS)),
            tri, wgt, wut, wdt, fg, l, l == depth - 1, tm_moe)
    return xf.reshape(batch, seq, D_MODEL)
```

```python
import functools

import jax
import jax.numpy as jnp
from jax import lax
from jax.experimental import pallas as pl
from jax.experimental.pallas import tpu as pltpu

F32 = jnp.float32
MXU_DTYPE = jnp.bfloat16

D_MODEL = 1024
CHUNK = 64
ATT_HEADS = 8
ATT_HEAD_DIM = 64
ATT_WIDTH = ATT_HEADS * ATT_HEAD_DIM
IDX_HEADS = 8
IDX_HEAD_DIM = 64
TOPK_MAX = 256
POOL_GROUPS = 4
POOL_GROUP_DIM = 64
POOL_WIDTH = POOL_GROUPS * POOL_GROUP_DIM
POOL_WINDOWS = (2, 4, 8, 16)
CONV_WIDTH = 256
CONV_KERNEL = 31
N_GROUPS = 4
EXPERTS_PER_GROUP = 4
N_EXPERTS = N_GROUPS * EXPERTS_PER_GROUP
EXPERT_HIDDEN = 512
EPS = 1e-6

LANES = 128
INT_MIN = -2 ** 31
NEG_INF = float("-inf")

N16 = ATT_WIDTH + 2 * IDX_HEAD_DIM
KI_COL_BLOCK = ATT_WIDTH // LANES
N32 = POOL_WIDTH + 2 * CONV_WIDTH
NT_ROWS = 3 * ATT_WIDTH
WI_ROWS = 16
VT_ROWS = ATT_HEAD_DIM + 16
KEY_NEG_INF = (0xFF800000 ^ 0x7FFFFFFF) - 2 ** 32

QB = 256
TK = 256
CNT_ROWS = 16
HALO = 32
VMEM_LIMIT = 48 * 1024 * 1024
MOE_VMEM_LIMIT = 56 * 1024 * 1024
RB = 320
SB = 256

_NT = (((1,), (1,)), ((), ()))


def _rms(x, g):
    return x * lax.rsqrt(jnp.mean(x * x, axis=-1, keepdims=True) + EPS) * g


def _proj_kernel(x_ref, g_ref, w_ref, wt_ref, wwit_ref, o16_ref, o32_ref, qt_ref, vt_ref, wit_ref, *, tm):
    h = _rms(x_ref[...], g_ref[...]).astype(MXU_DTYPE)
    p = jnp.dot(h, w_ref[...], preferred_element_type=F32)
    o16_ref[...] = p[:, :N16].astype(o16_ref.dtype)
    o32_ref[...] = p[:, N16:]
    pt = lax.dot_general(wt_ref[...], h, _NT, preferred_element_type=F32)
    qt_ref[...] = pt[:2 * ATT_WIDTH, :].astype(qt_ref.dtype)
    ones = jnp.ones((VT_ROWS - ATT_HEAD_DIM, TK), vt_ref.dtype)
    for c in range(tm // TK):
        for hd in range(ATT_HEADS):
            r0 = 2 * ATT_WIDTH + hd * ATT_HEAD_DIM
            vt_ref[c, hd * VT_ROWS:hd * VT_ROWS + ATT_HEAD_DIM, :] = (
                pt[r0:r0 + ATT_HEAD_DIM, c * TK:(c + 1) * TK].astype(vt_ref.dtype))
            vt_ref[c, hd * VT_ROWS + ATT_HEAD_DIM:(hd + 1) * VT_ROWS, :] = ones
    wit_ref[...] = lax.dot_general(wwit_ref[...], h, _NT, preferred_element_type=F32)


def _proj(x2d, g, w, wt, wwit, tm):
    t = x2d.shape[0]
    return pl.pallas_call(
        functools.partial(_proj_kernel, tm=tm),
        grid=(t // tm,),
        in_specs=[
            pl.BlockSpec((tm, D_MODEL), lambda i: (i, 0)),
            pl.BlockSpec((1, D_MODEL), lambda i: (0, 0)),
            pl.BlockSpec((D_MODEL, N16 + N32), lambda i: (0, 0)),
            pl.BlockSpec((NT_ROWS, D_MODEL), lambda i: (0, 0)),
            pl.BlockSpec((WI_ROWS, D_MODEL), lambda i: (0, 0)),
        ],
        out_specs=[
            pl.BlockSpec((tm, N16), lambda i: (i, 0)),
            pl.BlockSpec((tm, N32), lambda i: (i, 0)),
            pl.BlockSpec((2 * ATT_WIDTH, tm), lambda i: (0, i)),
            pl.BlockSpec((tm // TK, ATT_HEADS * VT_ROWS, TK), lambda i: (i, 0, 0)),
            pl.BlockSpec((WI_ROWS, tm), lambda i: (0, i)),
        ],
        out_shape=[
            jax.ShapeDtypeStruct((t, N16), MXU_DTYPE),
            jax.ShapeDtypeStruct((t, N32), F32),
            jax.ShapeDtypeStruct((2 * ATT_WIDTH, t), MXU_DTYPE),
            jax.ShapeDtypeStruct((t // TK, ATT_HEADS * VT_ROWS, TK), MXU_DTYPE),
            jax.ShapeDtypeStruct((WI_ROWS, t), F32),
        ],
        compiler_params=pltpu.CompilerParams(
            dimension_semantics=("parallel",), vmem_limit_bytes=VMEM_LIMIT),
        name="proj",
    )(x2d, g, w, wt, wwit)


def _attn_kernel(qt_ref, qit_ref, k_ref, ki_ref, vt_ref, wit_ref, o_ref,
                 qm_ref, qim_ref, tri_ref, sc_ref, lg_ref, p_ref, acc_ref, m_ref, l_ref, *, ktop):
    j = pl.program_id(1)
    nkt = j + 1

    row = lax.broadcasted_iota(jnp.int32, (LANES, QB), 0)
    for h in range(ATT_HEADS):
        pr, half = divmod(h, 2)
        keep = (row < ATT_HEAD_DIM) if half == 0 else (row >= ATT_HEAD_DIM)
        qp = qt_ref[pr * LANES:(pr + 1) * LANES, :].astype(F32) * (ATT_HEAD_DIM ** -0.5)
        qm_ref[h] = jnp.where(keep, qp, 0.0).astype(qm_ref.dtype)
        qip = qit_ref[pr * LANES:(pr + 1) * LANES, :].astype(F32) * (IDX_HEAD_DIM ** -0.5)
        qim_ref[h] = jnp.where(keep, qip, 0.0).astype(qim_ref.dtype)

    r_i = lax.broadcasted_iota(jnp.int32, (TK, TK), 0)
    c_i = lax.broadcasted_iota(jnp.int32, (TK, TK), 1)
    tri_ref[...] = jnp.where(c_i <= r_i, 1.0, 0.0).astype(tri_ref.dtype)

    wt = wit_ref[...]
    q_chunk = (j * QB + lax.broadcasted_iota(jnp.int32, (1, QB), 1)) // CHUNK

    def admissible(k0):
        k_chunk = (k0 + lax.broadcasted_iota(jnp.int32, (TK, 1), 0)) // CHUNK
        return k_chunk <= q_chunk

    def score_tile(kt, carry):
        k0 = pl.multiple_of(kt * TK, TK)
        kit = ki_ref[pl.ds(k0, TK), :]
        acc = jnp.zeros((TK, QB), F32)
        for h in range(IDX_HEADS):
            d = jnp.dot(kit, qim_ref[h], preferred_element_type=F32)
            acc = acc + jnp.maximum(d, 0.0) * wt[h:h + 1, :]
        sc_ref[kt // 2, pl.ds(pl.multiple_of((kt % 2) * TK, TK), TK), :] = (
            jnp.where(admissible(k0), acc * (IDX_HEADS ** -0.5), NEG_INF))
        return carry

    lax.fori_loop(0, nkt, score_tile, 0)

    @pl.when(nkt % 2 == 1)
    def _():
        sc_ref[nkt // 2, TK:, :] = jnp.full((TK, QB), NEG_INF, F32)

    def count(pred_fn):
        def body(kp, cnt):
            ones = jnp.where(pred_fn(sc_ref[kp]), 1.0, 0.0)
            return cnt + jnp.sum(ones.reshape(2 * TK // CNT_ROWS, CNT_ROWS, QB), axis=0)
        cnt = lax.fori_loop(0, (nkt + 1) // 2, body, jnp.zeros((CNT_ROWS, QB), F32))
        return jnp.sum(cnt, axis=0, keepdims=True)

    def as_float(u):
        key = u ^ INT_MIN
        bits = key ^ ((key >> 31) & 0x7FFFFFFF)
        return jnp.where(key < KEY_NEG_INF, NEG_INF, lax.bitcast_convert_type(bits, F32))

    def bit_step(i, prefix):
        cand_u = prefix | lax.shift_left(jnp.int32(1), 31 - i)
        cand = as_float(cand_u)
        total = count(lambda s: s >= cand)
        return jnp.where(total >= ktop, cand_u, prefix)

    prefix = lax.fori_loop(0, 32, bit_step, jnp.zeros((1, QB), jnp.int32))
    thr = as_float(prefix)
    need = ktop - count(lambda s: s > thr)

    m_ref[...] = jnp.full(m_ref.shape, NEG_INF, F32)
    l_ref[...] = jnp.zeros(l_ref.shape, F32)
    acc_ref[...] = jnp.zeros(acc_ref.shape, F32)

    def attend_tile(kt, eq_before):
        k0 = pl.multiple_of(kt * TK, TK)
        s = sc_ref[kt // 2, pl.ds(pl.multiple_of((kt % 2) * TK, TK), TK), :]
        eq = s == thr
        eqf = jnp.where(eq, 1.0, 0.0)
        incl = jnp.dot(tri_ref[...], eqf.astype(tri_ref.dtype), preferred_element_type=F32)
        sel = ((s > thr) | (eq & ((eq_before + incl) <= need))) & admissible(k0)
        bias = jnp.where(sel, 0.0, NEG_INF)
        alphas = []
        for h in range(ATT_HEADS):
            pr = h // 2
            kp = k_ref[pl.ds(k0, TK), pr * LANES:(pr + 1) * LANES]
            lg = jnp.dot(kp, qm_ref[h], preferred_element_type=F32) + bias
            lg_ref[h] = lg
            m_old = m_ref[h:h + 1, :]
            m_tile = jnp.max(lg.reshape(TK // CNT_ROWS, CNT_ROWS, QB), axis=0)
            m_new = jnp.maximum(m_old, jnp.max(m_tile, axis=0, keepdims=True))
            m_ref[h:h + 1, :] = m_new
            m_safe = jnp.where(m_new == NEG_INF, 0.0, m_new)
            alphas.append((jnp.exp(m_old - m_safe), m_safe))
        for h in range(ATT_HEADS):
            alpha, m_safe = alphas[h]
            p_ref[h] = jnp.exp(lg_ref[h] - m_safe).astype(p_ref.dtype)
        for h in range(ATT_HEADS):
            alpha = alphas[h][0]
            pv = jnp.dot(vt_ref[kt, h * VT_ROWS:(h + 1) * VT_ROWS, :], p_ref[h],
                         preferred_element_type=F32)
            rows = slice(h * ATT_HEAD_DIM, (h + 1) * ATT_HEAD_DIM)
            acc_ref[rows, :] = alpha * acc_ref[rows, :] + pv[:ATT_HEAD_DIM, :]
            l_ref[h:h + 1, :] = alpha * l_ref[h:h + 1, :] + pv[ATT_HEAD_DIM:ATT_HEAD_DIM + 1, :]
        return eq_before + jnp.sum(eqf, axis=0, keepdims=True)

    lax.fori_loop(0, nkt, attend_tile, jnp.zeros((1, QB), F32))

    for h in range(ATT_HEADS):
        rows = slice(h * ATT_HEAD_DIM, (h + 1) * ATT_HEAD_DIM)
        acc_ref[rows, :] = acc_ref[rows, :] / l_ref[h:h + 1, :]
    o_ref[...] = acc_ref[...].T.astype(o_ref.dtype)


def _attn(qt, o16, vt, wit, batch, seq):
    t = batch * seq
    nq = seq // QB
    nkt = seq // TK
    ktop = min(TOPK_MAX, seq // 4)
    return pl.pallas_call(
        functools.partial(_attn_kernel, ktop=ktop),
        grid=(batch, nq),
        in_specs=[
            pl.BlockSpec((ATT_WIDTH, QB), lambda b, j: (0, b * nq + j)),
            pl.BlockSpec((ATT_WIDTH, QB), lambda b, j: (1, b * nq + j)),
            pl.BlockSpec((seq, ATT_WIDTH), lambda b, j: (b, 0)),
            pl.BlockSpec((seq, LANES), lambda b, j: (b, KI_COL_BLOCK)),
            pl.BlockSpec((nkt, ATT_HEADS * VT_ROWS, TK), lambda b, j: (b, 0, 0)),
            pl.BlockSpec((WI_ROWS, QB), lambda b, j: (0, b * nq + j)),
        ],
        out_specs=pl.BlockSpec((QB, ATT_WIDTH), lambda b, j: (b * nq + j, 0)),
        out_shape=jax.ShapeDtypeStruct((t, ATT_WIDTH), MXU_DTYPE),
        scratch_shapes=[
            pltpu.VMEM((ATT_HEADS, LANES, QB), MXU_DTYPE),
            pltpu.VMEM((IDX_HEADS, LANES, QB), MXU_DTYPE),
            pltpu.VMEM((TK, TK), MXU_DTYPE),
            pltpu.VMEM(((nkt + 1) // 2, 2 * TK, QB), F32),
            pltpu.VMEM((ATT_HEADS, TK, QB), F32),
            pltpu.VMEM((ATT_HEADS, TK, QB), MXU_DTYPE),
            pltpu.VMEM((ATT_WIDTH, QB), F32),
            pltpu.VMEM((ATT_HEADS, QB), F32),
            pltpu.VMEM((ATT_HEADS, QB), F32),
        ],
        compiler_params=pltpu.CompilerParams(
            dimension_semantics=("parallel", "parallel"), vmem_limit_bytes=VMEM_LIMIT),
        name="attn",
    )(qt, qt, o16, o16, vt, wit)


def _mix_kernel(x_ref, ya_ref, cur_ref, halo_ref, wp_ref, ps_ref, dw_ref, dwb_ref, lng_ref, lnb_ref,
                pw_ref, pwb_ref, wo_ref, o_ref, ubuf, hbuf, *, tm):
    j = pl.program_id(1)
    cur = cur_ref[...]
    halo = jnp.where(j > 0, halo_ref[...], 0.0)

    def glu(z):
        return z[:, POOL_WIDTH:POOL_WIDTH + CONV_WIDTH] * jax.nn.sigmoid(z[:, POOL_WIDTH + CONV_WIDTH:])

    u = cur[:, :POOL_WIDTH]
    ubuf[0:HALO, :] = halo[:, :POOL_WIDTH]
    ubuf[HALO:, :] = u
    hbuf[0:HALO, :] = glu(halo)
    hbuf[HALO:, :] = glu(cur)

    lane = lax.broadcasted_iota(jnp.int32, (tm, LANES), 1)
    upper = lane >= POOL_GROUP_DIM
    s0 = u[:, :LANES]
    s1 = u[:, LANES:]
    for i in range(1, POOL_WINDOWS[3]):
        if i < POOL_WINDOWS[1]:
            sh = ubuf[HALO - i:HALO - i + tm, 0:LANES]
            s0 = s0 + (sh if i < POOL_WINDOWS[0] else jnp.where(upper, sh, 0.0))
        sh = ubuf[HALO - i:HALO - i + tm, LANES:2 * LANES]
        s1 = s1 + (sh if i < POOL_WINDOWS[2] else jnp.where(upper, sh, 0.0))
    t1 = (j * tm + lax.broadcasted_iota(jnp.int32, (tm, LANES), 0) + 1).astype(F32)
    w0 = jnp.where(upper, float(POOL_WINDOWS[1]), float(POOL_WINDOWS[0]))
    w1 = jnp.where(upper, float(POOL_WINDOWS[3]), float(POOL_WINDOWS[2]))
    pooled = jnp.concatenate([s0 / jnp.minimum(t1, w0), s1 / jnp.minimum(t1, w1)], axis=1)
    d = (pooled - u).astype(MXU_DTYPE)
    yb = jnp.dot(d, wp_ref[...], preferred_element_type=F32) * ps_ref[...]

    c = jnp.zeros((tm, CONV_WIDTH), F32) + dwb_ref[...]
    off = HALO - (CONV_KERNEL - 1)
    for jj in range(CONV_KERNEL):
        c = c + hbuf[off + jj:off + jj + tm, :] * dw_ref[jj:jj + 1, :]
    mu = jnp.mean(c, axis=-1, keepdims=True)
    cc = c - mu
    var = jnp.mean(cc * cc, axis=-1, keepdims=True)
    hn = cc * lax.rsqrt(var + EPS) * lng_ref[...] + lnb_ref[...]
    sw = (hn * jax.nn.sigmoid(hn)).astype(MXU_DTYPE)
    yc = jnp.dot(sw, pw_ref[...], preferred_element_type=F32) + pwb_ref[...]

    y = jnp.dot(ya_ref[...], wo_ref[0:ATT_WIDTH, :], preferred_element_type=F32)
    y = y + jnp.dot(yb.astype(MXU_DTYPE), wo_ref[ATT_WIDTH:ATT_WIDTH + POOL_WIDTH, :], preferred_element_type=F32)
    y = y + jnp.dot(yc.astype(MXU_DTYPE), wo_ref[ATT_WIDTH + POOL_WIDTH:, :], preferred_element_type=F32)
    o_ref[...] = x_ref[...] + y


def _mix(x2d, ya, o32, wp, ps, dw, dwb, lng, lnb, pw, pwb, wo, batch, seq, tm):
    t = batch * seq
    nt = seq // tm
    hb = tm // HALO
    full = lambda b, j: (0, 0)
    return pl.pallas_call(
        functools.partial(_mix_kernel, tm=tm),
        grid=(batch, nt),
        in_specs=[
            pl.BlockSpec((tm, D_MODEL), lambda b, j: (b * nt + j, 0)),
            pl.BlockSpec((tm, ATT_WIDTH), lambda b, j: (b * nt + j, 0)),
            pl.BlockSpec((tm, N32), lambda b, j: (b * nt + j, 0)),
            pl.BlockSpec((HALO, N32), lambda b, j: (jnp.maximum((b * nt + j) * hb - 1, 0), 0)),
            pl.BlockSpec((POOL_WIDTH, POOL_WIDTH), full),
            pl.BlockSpec((1, POOL_WIDTH), full),
            pl.BlockSpec((HALO, CONV_WIDTH), full),
            pl.BlockSpec((1, CONV_WIDTH), full),
            pl.BlockSpec((1, CONV_WIDTH), full),
            pl.BlockSpec((1, CONV_WIDTH), full),
            pl.BlockSpec((CONV_WIDTH, CONV_WIDTH), full),
            pl.BlockSpec((1, CONV_WIDTH), full),
            pl.BlockSpec((D_MODEL, D_MODEL), full),
        ],
        out_specs=pl.BlockSpec((tm, D_MODEL), lambda b, j: (b * nt + j, 0)),
        out_shape=jax.ShapeDtypeStruct((t, D_MODEL), F32),
        scratch_shapes=[
            pltpu.VMEM((HALO + tm, POOL_WIDTH), F32),
            pltpu.VMEM((HALO + tm, CONV_WIDTH), F32),
        ],
        compiler_params=pltpu.CompilerParams(
            dimension_semantics=("parallel", "parallel"), vmem_limit_bytes=VMEM_LIMIT),
        name="mix",
    )(x2d, ya, o32, o32, wp, ps, dw, dwb, lng, lnb, pw, pwb, wo)


def _route(glog, elog):
    lane = lax.broadcasted_iota(jnp.int32, glog.shape, 1)
    lane_f = lane.astype(F32)
    big = float(LANES)
    gl = jnp.where(lane < N_GROUPS, glog, NEG_INF)
    ge = jnp.exp(gl - jnp.max(gl, axis=-1, keepdims=True))
    gp = ge / jnp.sum(ge, axis=-1, keepdims=True)
    p_g = jnp.max(gp, axis=-1, keepdims=True)
    g_sel = jnp.min(jnp.where(gp == p_g, lane_f, big), axis=-1, keepdims=True)
    in_grp = (lane // EXPERTS_PER_GROUP).astype(F32) == g_sel
    el = jnp.where(in_grp, elog, NEG_INF)
    ee = jnp.exp(el - jnp.max(el, axis=-1, keepdims=True))
    ep = ee / jnp.sum(ee, axis=-1, keepdims=True)
    ep = jnp.where(in_grp, ep, -1.0)
    v1 = jnp.max(ep, axis=-1, keepdims=True)
    i1 = jnp.min(jnp.where(ep == v1, lane_f, big), axis=-1, keepdims=True)
    ep2 = jnp.where(lane_f == i1, -1.0, ep)
    v2 = jnp.max(ep2, axis=-1, keepdims=True)
    i2 = jnp.min(jnp.where(ep2 == v2, lane_f, big), axis=-1, keepdims=True)
    den = v1 + v2
    w_e = jnp.where(lane_f == i1, v1 / den, jnp.where(lane_f == i2, v2 / den, 0.0))
    return p_g * w_e, g_sel


def _moe_kernel(x_ref, g2_ref, rgw_ref, rgb_ref, rew_ref, reb_ref, tri_ref, wg_ref, wu_ref, wd_ref, fg_ref,
                o_ref, h_ref, gate3_ref, info_ref, infot_ref, xg_ref, gg_ref, yg_ref, cnt_ref,
                *, tm, final_norm):
    e = pl.program_id(1)
    grp = e // EXPERTS_PER_GROUP
    le = e % EXPERTS_PER_GROUP
    grp_f = grp.astype(F32)

    @pl.when(e == 0)
    def _():
        x = x_ref[...]
        hb = _rms(x, g2_ref[...]).astype(MXU_DTYPE)
        h_ref[...] = hb
        glog = jnp.dot(hb, rgw_ref[...], preferred_element_type=F32) + rgb_ref[...]
        elog = jnp.dot(hb, rew_ref[...], preferred_element_type=F32) + reb_ref[...]
        gate, g_sel = _route(glog, elog)
        g1 = gate.astype(MXU_DTYPE)
        r1 = gate - g1.astype(F32)
        g2 = r1.astype(MXU_DTYPE)
        gate3_ref[0] = g1
        gate3_ref[1] = g2
        gate3_ref[2] = (r1 - g2.astype(F32)).astype(MXU_DTYPE)
        lane = lax.broadcasted_iota(jnp.int32, (tm, LANES), 1)
        member = jnp.where(lane.astype(F32) == g_sel, 1.0, 0.0).astype(MXU_DTYPE)
        ranks = jnp.dot(tri_ref[...], member, preferred_element_type=F32)
        info = jnp.where(lane < N_GROUPS, ranks, jnp.where(lane == N_GROUPS, g_sel, 0.0))
        info_ref[...] = info
        infot_ref[...] = info.T[:8, :]
        for gi in range(N_GROUPS):
            cnt_ref[gi] = ranks[tm - 1, gi].astype(jnp.int32)
        o_ref[...] = x

    n_rows = cnt_ref[grp]
    n_blk = (n_rows + (RB - 1)) // RB

    @pl.when(le == 0)
    def _():
        rank_t = infot_ref[pl.ds(grp, 1), :]
        member_t = infot_ref[N_GROUPS:N_GROUPS + 1, :] == grp_f

        def gather(rb, carry):
            r0 = pl.multiple_of(rb * RB, RB)
            want = (r0 + 1 + lax.broadcasted_iota(jnp.int32, (RB, tm), 0)).astype(F32)
            pick = jnp.where(member_t & (rank_t == want), 1.0, 0.0).astype(MXU_DTYPE)
            xg_ref[pl.ds(r0, RB), :] = jnp.dot(pick, h_ref[...], preferred_element_type=F32).astype(xg_ref.dtype)
            gg_ref[pl.ds(r0, RB), :] = (jnp.dot(pick, gate3_ref[0], preferred_element_type=F32)
                                        + jnp.dot(pick, gate3_ref[1], preferred_element_type=F32)
                                        + jnp.dot(pick, gate3_ref[2], preferred_element_type=F32))
            return carry

        lax.fori_loop(0, n_blk, gather, 0)
        yg_ref[...] = jnp.zeros(yg_ref.shape, F32)

    wg = wg_ref[0]
    wu = wu_ref[0]
    wd = wd_ref[0]

    def expert(rb, carry):
        r0 = pl.multiple_of(rb * RB, RB)
        xb = xg_ref[pl.ds(r0, RB), :]
        a = jnp.dot(xb, wg, preferred_element_type=F32)
        b = jnp.dot(xb, wu, preferred_element_type=F32)
        lane = lax.broadcasted_iota(jnp.int32, (RB, LANES), 1)
        g_col = jnp.sum(jnp.where(lane == e, gg_ref[pl.ds(r0, RB), :], 0.0), axis=-1, keepdims=True)
        act = (a * jax.nn.sigmoid(a)) * b * g_col
        yg_ref[pl.ds(r0, RB), :] += jnp.dot(act.astype(MXU_DTYPE), wd, preferred_element_type=F32)
        return carry

    lax.fori_loop(0, n_blk, expert, 0)

    @pl.when(le == EXPERTS_PER_GROUP - 1)
    def _():
        lane = lax.broadcasted_iota(jnp.int32, (tm, LANES), 1)
        info = info_ref[...]
        rank_c = jnp.sum(jnp.where(lane == grp, info, 0.0), axis=-1, keepdims=True)
        member_c = jnp.sum(jnp.where(lane == N_GROUPS, info, 0.0), axis=-1, keepdims=True) == grp_f

        def scatter(sb, carry):
            r0 = pl.multiple_of(sb * SB, SB)
            want = (r0 + 1 + lax.broadcasted_iota(jnp.int32, (tm, SB), 1)).astype(F32)
            put = jnp.where(member_c & (rank_c == want), 1.0, 0.0).astype(MXU_DTYPE)
            y = yg_ref[pl.ds(r0, SB), :]
            y_hi = y.astype(MXU_DTYPE)
            y_lo = (y - y_hi.astype(F32)).astype(MXU_DTYPE)
            o_ref[...] += (jnp.dot(put, y_hi, preferred_element_type=F32)
                           + jnp.dot(put, y_lo, preferred_element_type=F32))
            return carry

        lax.fori_loop(0, (n_rows + (SB - 1)) // SB, scatter, 0)

    if final_norm:
        @pl.when(e == N_EXPERTS - 1)
        def _():
            o_ref[...] = _rms(o_ref[...], fg_ref[...])


def _moe(x2d, g2, rgw, rgb, rew, reb, tri, wg, wu, wd, fg, layer, final_norm, tm):
    t = x2d.shape[0]
    cap = pl.cdiv(tm, RB) * RB
    full = lambda i, e: (0, 0)
    return pl.pallas_call(
        functools.partial(_moe_kernel, tm=tm, final_norm=final_norm),
        grid=(t // tm, N_EXPERTS),
        in_specs=[
            pl.BlockSpec((tm, D_MODEL), lambda i, e: (i, 0)),
            pl.BlockSpec((1, D_MODEL), full),
            pl.BlockSpec((D_MODEL, LANES), full),
            pl.BlockSpec((1, LANES), full),
            pl.BlockSpec((D_MODEL, LANES), full),
            pl.BlockSpec((1, LANES), full),
            pl.BlockSpec((tm, tm), full),
            pl.BlockSpec((1, D_MODEL, EXPERT_HIDDEN), lambda i, e: (layer * N_EXPERTS + e, 0, 0)),
            pl.BlockSpec((1, D_MODEL, EXPERT_HIDDEN), lambda i, e: (layer * N_EXPERTS + e, 0, 0)),
            pl.BlockSpec((1, EXPERT_HIDDEN, D_MODEL), lambda i, e: (layer * N_EXPERTS + e, 0, 0)),
            pl.BlockSpec((1, D_MODEL), full),
        ],
        out_specs=pl.BlockSpec((tm, D_MODEL), lambda i, e: (i, 0)),
        out_shape=jax.ShapeDtypeStruct((t, D_MODEL), F32),
        scratch_shapes=[
            pltpu.VMEM((tm, D_MODEL), MXU_DTYPE),
            pltpu.VMEM((3, tm, LANES), MXU_DTYPE),
            pltpu.VMEM((tm, LANES), F32),
            pltpu.VMEM((8, tm), F32),
            pltpu.VMEM((cap, D_MODEL), MXU_DTYPE),
            pltpu.VMEM((cap, LANES), F32),
            pltpu.VMEM((cap, D_MODEL), F32),
            pltpu.SMEM((N_GROUPS,), jnp.int32),
        ],
        compiler_params=pltpu.CompilerParams(
            dimension_semantics=("parallel", "arbitrary"), vmem_limit_bytes=MOE_VMEM_LIMIT),
        name="moe",
    )(x2d, g2, rgw, rgb, rew, reb, tri, wg, wu, wd, fg)


def _pad_lanes(w):
    return jnp.pad(w, ((0, 0), (0, LANES - w.shape[-1])))


def _block_diag(blocks):
    g, n, _ = blocks.shape
    out = jnp.zeros((g * n, g * n), blocks.dtype)
    for i in range(g):
        out = out.at[i * n:(i + 1) * n, i * n:(i + 1) * n].set(blocks[i])
    return out


def kernel(x, norm1_g, w_in, pool_w, pool_scale, dw_w, dw_b, conv_ln_g, conv_ln_b, pw_w, pw_b,
           w_out, norm2_g, rg_w, rg_b, re_w, re_b, w_gate, w_up, w_down, final_g):
    batch, seq, d = x.shape
    depth = w_in.shape[0]
    t = batch * seq
    tm = min(512, seq)
    tm_moe = min(1024, t)
    assert d == D_MODEL and seq % QB == 0 and seq % tm == 0 and tm % TK == 0 and t % tm_moe == 0

    o_q, o_k, o_v = 0, ATT_WIDTH, 2 * ATT_WIDTH
    o_qi = 3 * ATT_WIDTH
    o_ki = o_qi + IDX_HEADS * IDX_HEAD_DIM
    o_wi = o_ki + IDX_HEAD_DIM
    o_pool = o_wi + IDX_HEADS
    o_conv = o_pool + POOL_WIDTH

    wg = w_gate.reshape(depth * N_EXPERTS, D_MODEL, EXPERT_HIDDEN).astype(MXU_DTYPE)
    wu = w_up.reshape(depth * N_EXPERTS, D_MODEL, EXPERT_HIDDEN).astype(MXU_DTYPE)
    wd = w_down.reshape(depth * N_EXPERTS, EXPERT_HIDDEN, D_MODEL).astype(MXU_DTYPE)
    fg = final_g.reshape(1, D_MODEL)
    tri = jnp.tri(tm_moe, dtype=MXU_DTYPE)

    xf = x.reshape(t, D_MODEL)
    for l in range(depth):
        w = w_in[l]
        w_ki = w[:, o_ki:o_wi]
        w_cat = jnp.concatenate([w[:, o_k:o_v], w_ki, w_ki, w[:, o_pool:]], axis=1).astype(MXU_DTYPE)
        wt = jnp.concatenate([w[:, o_q:o_k], w[:, o_qi:o_ki], w[:, o_v:o_qi]], axis=1).T.astype(MXU_DTYPE)
        wwit = jnp.pad(w[:, o_wi:o_pool].T, ((0, WI_ROWS - IDX_HEADS), (0, 0))).astype(MXU_DTYPE)
        o16, o32, qt, vt, wit = _proj(xf, norm1_g[l].reshape(1, D_MODEL), w_cat, wt, wwit, tm)

        ya = _attn(qt, o16, vt, wit, batch, seq)

        x1 = _mix(
            xf, ya, o32,
            _block_diag(pool_w[l]).astype(MXU_DTYPE), pool_scale[l].reshape(1, POOL_WIDTH),
            jnp.pad(dw_w[l], ((0, HALO - CONV_KERNEL), (0, 0))), dw_b[l].reshape(1, CONV_WIDTH),
            conv_ln_g[l].reshape(1, CONV_WIDTH), conv_ln_b[l].reshape(1, CONV_WIDTH),
            pw_w[l].astype(MXU_DTYPE), pw_b[l].reshape(1, CONV_WIDTH),
            w_out[l].astype(MXU_DTYPE), batch, seq, tm)

        xf = _moe(
            x1, norm2_g[l].reshape(1, D_MODEL),
            _pad_lanes(rg_w[l]).astype(MXU_DTYPE), _pad_lanes(rg_b[l].reshape(1, N_GROUPS)),
            _pad_lanes(re_w[l]).astype(MXU_DTYPE), _pad_lanes(re_b[l].reshape(1, N_EXPERTS)),
            tri, wg, wu, wd, fg, l, l == depth - 1, tm_moe)
    return xf.reshape(batch, seq, D_MODEL)
```

```python
import functools

import jax
import jax.numpy as jnp
from jax import lax
from jax.experimental import pallas as pl
from jax.experimental.pallas import tpu as pltpu

F32 = jnp.float32
MXU_DTYPE = jnp.bfloat16

D_MODEL = 1024
CHUNK = 64
ATT_HEADS = 8
ATT_HEAD_DIM = 64
ATT_WIDTH = ATT_HEADS * ATT_HEAD_DIM
IDX_HEADS = 8
IDX_HEAD_DIM = 64
TOPK_MAX = 256
POOL_GROUPS = 4
POOL_GROUP_DIM = 64
POOL_WIDTH = POOL_GROUPS * POOL_GROUP_DIM
POOL_WINDOWS = (2, 4, 8, 16)
CONV_WIDTH = 256
CONV_KERNEL = 31
N_GROUPS = 4
EXPERTS_PER_GROUP = 4
N_EXPERTS = N_GROUPS * EXPERTS_PER_GROUP
EXPERT_HIDDEN = 512
EPS = 1e-6

LANES = 128
SUBLANES = 8
INT_MIN = -2 ** 31
NEG_INF = float("-inf")

N16 = ATT_WIDTH + 2 * IDX_HEAD_DIM
KI_COL_BLOCK = ATT_WIDTH // LANES
N32 = POOL_WIDTH + 2 * CONV_WIDTH
NT_ROWS = 3 * ATT_WIDTH
WI_ROWS = 16
VT_ROWS = ATT_HEAD_DIM + 16
KEY_NEG_INF = (0xFF800000 ^ 0x7FFFFFFF) - 2 ** 32

QB = 512
TK = 256
HEAD_GROUP = 8
SEARCH_COLS = 256
CNT_ROWS = 16
HALO = 32
VMEM_LIMIT = 48 * 1024 * 1024
MOE_VMEM_LIMIT = 56 * 1024 * 1024
RB = 320
SB = 256

_NT = (((1,), (1,)), ((), ()))


def _rms(x, g):
    return x * lax.rsqrt(jnp.mean(x * x, axis=-1, keepdims=True) + EPS) * g


def _proj_kernel(x_ref, g_ref, w_ref, wt_ref, wwit_ref, o16_ref, o32_ref, qt_ref, vt_ref, wit_ref, *, tm):
    h = _rms(x_ref[...], g_ref[...]).astype(MXU_DTYPE)
    p = jnp.dot(h, w_ref[...], preferred_element_type=F32)
    o16_ref[...] = p[:, :N16].astype(o16_ref.dtype)
    o32_ref[...] = p[:, N16:]
    pt = lax.dot_general(wt_ref[...], h, _NT, preferred_element_type=F32)
    qt_ref[...] = pt[:2 * ATT_WIDTH, :].astype(qt_ref.dtype)
    ones = jnp.ones((VT_ROWS - ATT_HEAD_DIM, TK), vt_ref.dtype)
    for c in range(tm // TK):
        for hd in range(ATT_HEADS):
            r0 = 2 * ATT_WIDTH + hd * ATT_HEAD_DIM
            vt_ref[c, hd * VT_ROWS:hd * VT_ROWS + ATT_HEAD_DIM, :] = (
                pt[r0:r0 + ATT_HEAD_DIM, c * TK:(c + 1) * TK].astype(vt_ref.dtype))
            vt_ref[c, hd * VT_ROWS + ATT_HEAD_DIM:(hd + 1) * VT_ROWS, :] = ones
    wit_ref[...] = lax.dot_general(wwit_ref[...], h, _NT, preferred_element_type=F32)


def _proj(x2d, g, w, wt, wwit, tm):
    t = x2d.shape[0]
    return pl.pallas_call(
        functools.partial(_proj_kernel, tm=tm),
        grid=(t // tm,),
        in_specs=[
            pl.BlockSpec((tm, D_MODEL), lambda i: (i, 0)),
            pl.BlockSpec((1, D_MODEL), lambda i: (0, 0)),
            pl.BlockSpec((D_MODEL, N16 + N32), lambda i: (0, 0)),
            pl.BlockSpec((NT_ROWS, D_MODEL), lambda i: (0, 0)),
            pl.BlockSpec((WI_ROWS, D_MODEL), lambda i: (0, 0)),
        ],
        out_specs=[
            pl.BlockSpec((tm, N16), lambda i: (i, 0)),
            pl.BlockSpec((tm, N32), lambda i: (i, 0)),
            pl.BlockSpec((2 * ATT_WIDTH, tm), lambda i: (0, i)),
            pl.BlockSpec((tm // TK, ATT_HEADS * VT_ROWS, TK), lambda i: (i, 0, 0)),
            pl.BlockSpec((WI_ROWS, tm), lambda i: (0, i)),
        ],
        out_shape=[
            jax.ShapeDtypeStruct((t, N16), MXU_DTYPE),
            jax.ShapeDtypeStruct((t, N32), F32),
            jax.ShapeDtypeStruct((2 * ATT_WIDTH, t), MXU_DTYPE),
            jax.ShapeDtypeStruct((t // TK, ATT_HEADS * VT_ROWS, TK), MXU_DTYPE),
            jax.ShapeDtypeStruct((WI_ROWS, t), F32),
        ],
        compiler_params=pltpu.CompilerParams(
            dimension_semantics=("parallel",), vmem_limit_bytes=VMEM_LIMIT),
        name="proj",
    )(x2d, g, w, wt, wwit)


def _attn_kernel(qt_ref, qit_ref, k_ref, ki_ref, vt_ref, wit_ref, o_ref,
                 qm_ref, qim_ref, tri_ref, sc_ref, lg_ref, p_ref, acc_ref, m_ref, l_ref, *, ktop):
    j = pl.program_id(1)
    nkt = (j + 1) * (QB // TK)

    row = lax.broadcasted_iota(jnp.int32, (LANES, QB), 0)
    for h in range(ATT_HEADS):
        pr, half = divmod(h, 2)
        keep = (row < ATT_HEAD_DIM) if half == 0 else (row >= ATT_HEAD_DIM)
        qp = qt_ref[pr * LANES:(pr + 1) * LANES, :].astype(F32) * (ATT_HEAD_DIM ** -0.5)
        qm_ref[h] = jnp.where(keep, qp, 0.0).astype(qm_ref.dtype)
        qip = qit_ref[pr * LANES:(pr + 1) * LANES, :].astype(F32) * (IDX_HEAD_DIM ** -0.5)
        qim_ref[h] = jnp.where(keep, qip, 0.0).astype(qim_ref.dtype)

    r_i = lax.broadcasted_iota(jnp.int32, (TK, TK), 0)
    c_i = lax.broadcasted_iota(jnp.int32, (TK, TK), 1)
    tri_ref[...] = jnp.where(c_i <= r_i, 1.0, 0.0).astype(tri_ref.dtype)

    wt = wit_ref[...]
    q_chunk = (j * QB + lax.broadcasted_iota(jnp.int32, (1, QB), 1)) // CHUNK

    def admissible(k0):
        k_chunk = (k0 + lax.broadcasted_iota(jnp.int32, (TK, 1), 0)) // CHUNK
        return k_chunk <= q_chunk

    def score_tile(kt, carry):
        k0 = pl.multiple_of(kt * TK, TK)
        kit = ki_ref[pl.ds(k0, TK), :]
        acc = jnp.zeros((TK, QB), F32)
        for h in range(IDX_HEADS):
            d = jnp.dot(kit, qim_ref[h], preferred_element_type=F32)
            acc = acc + jnp.maximum(d, 0.0) * wt[h:h + 1, :]
        sc_ref[kt // 2, pl.ds(pl.multiple_of((kt % 2) * TK, TK), TK), :] = (
            jnp.where(admissible(k0), acc * (IDX_HEADS ** -0.5), NEG_INF))
        return carry

    lax.fori_loop(0, nkt, score_tile, 0)

    @pl.when(nkt % 2 == 1)
    def _():
        sc_ref[nkt // 2, TK:, :] = jnp.full((TK, QB), NEG_INF, F32)

    def count(pred_fn, c0):
        def body(kp, cnt):
            ones = jnp.where(pred_fn(sc_ref[kp, :, c0:c0 + SEARCH_COLS]), 1.0, 0.0)
            return cnt + jnp.sum(ones.reshape(2 * TK // CNT_ROWS, CNT_ROWS, SEARCH_COLS), axis=0)
        cnt = lax.fori_loop(0, (nkt + 1) // 2, body, jnp.zeros((CNT_ROWS, SEARCH_COLS), F32))
        return jnp.sum(cnt, axis=0, keepdims=True)

    def as_float(u):
        key = u ^ INT_MIN
        bits = key ^ ((key >> 31) & 0x7FFFFFFF)
        return jnp.where(key < KEY_NEG_INF, NEG_INF, lax.bitcast_convert_type(bits, F32))

    thr_parts, need_parts = [], []
    for c0 in range(0, QB, SEARCH_COLS):
        def bit_step(i, prefix, c0=c0):
            cand_u = prefix | lax.shift_left(jnp.int32(1), 31 - i)
            cand = as_float(cand_u)
            total = count(lambda s: s >= cand, c0)
            return jnp.where(total >= ktop, cand_u, prefix)

        prefix = lax.fori_loop(0, 32, bit_step, jnp.zeros((1, SEARCH_COLS), jnp.int32))
        thr_c = as_float(prefix)
        thr_parts.append(thr_c)
        need_parts.append(ktop - count(lambda s: s > thr_c, c0))
    thr = jnp.concatenate(thr_parts, axis=1)
    need = jnp.concatenate(need_parts, axis=1)

    m_ref[...] = jnp.full(m_ref.shape, NEG_INF, F32)
    l_ref[...] = jnp.zeros(l_ref.shape, F32)
    acc_ref[...] = jnp.zeros(acc_ref.shape, F32)

    def attend_tile(kt, eq_before):
        k0 = pl.multiple_of(kt * TK, TK)
        s = sc_ref[kt // 2, pl.ds(pl.multiple_of((kt % 2) * TK, TK), TK), :]
        eq = s == thr
        eqf = jnp.where(eq, 1.0, 0.0)
        incl = jnp.dot(tri_ref[...], eqf.astype(tri_ref.dtype), preferred_element_type=F32)
        sel = ((s > thr) | (eq & ((eq_before + incl) <= need))) & admissible(k0)
        bias = jnp.where(sel, 0.0, NEG_INF)
        for h0 in range(0, ATT_HEADS, HEAD_GROUP):
            heads = range(h0, h0 + HEAD_GROUP)
            alphas = {}
            for h in heads:
                pr = h // 2
                kp = k_ref[pl.ds(k0, TK), pr * LANES:(pr + 1) * LANES]
                lg = jnp.dot(kp, qm_ref[h], preferred_element_type=F32) + bias
                lg_ref[h] = lg
                m_old = m_ref[h:h + 1, :]
                m_tile = jnp.max(lg.reshape(TK // CNT_ROWS, CNT_ROWS, QB), axis=0)
                m_new = jnp.maximum(m_old, jnp.max(m_tile, axis=0, keepdims=True))
                m_ref[h:h + 1, :] = m_new
                m_safe = jnp.where(m_new == NEG_INF, 0.0, m_new)
                alphas[h] = (jnp.exp(m_old - m_safe), m_safe)
            for h in heads:
                p_ref[h] = jnp.exp(lg_ref[h] - alphas[h][1]).astype(p_ref.dtype)
            for h in heads:
                alpha = alphas[h][0]
                pv = jnp.dot(vt_ref[kt, h * VT_ROWS:(h + 1) * VT_ROWS, :], p_ref[h],
                             preferred_element_type=F32)
                rows = slice(h * ATT_HEAD_DIM, (h + 1) * ATT_HEAD_DIM)
                acc_ref[rows, :] = alpha * acc_ref[rows, :] + pv[:ATT_HEAD_DIM, :]
                l_ref[h:h + 1, :] = alpha * l_ref[h:h + 1, :] + pv[ATT_HEAD_DIM:ATT_HEAD_DIM + 1, :]
        return eq_before + jnp.sum(eqf, axis=0, keepdims=True)

    lax.fori_loop(0, nkt, attend_tile, jnp.zeros((1, QB), F32))

    for h in range(ATT_HEADS):
        rows = slice(h * ATT_HEAD_DIM, (h + 1) * ATT_HEAD_DIM)
        acc_ref[rows, :] = acc_ref[rows, :] / l_ref[h:h + 1, :]
    o_ref[...] = acc_ref[...].T.astype(o_ref.dtype)


def _attn(qt, o16, vt, wit, batch, seq):
    t = batch * seq
    nq = seq // QB
    nkt = seq // TK
    ktop = min(TOPK_MAX, seq // 4)
    return pl.pallas_call(
        functools.partial(_attn_kernel, ktop=ktop),
        grid=(batch, nq),
        in_specs=[
            pl.BlockSpec((ATT_WIDTH, QB), lambda b, j: (0, b * nq + j)),
            pl.BlockSpec((ATT_WIDTH, QB), lambda b, j: (1, b * nq + j)),
            pl.BlockSpec((seq, ATT_WIDTH), lambda b, j: (b, 0)),
            pl.BlockSpec((seq, LANES), lambda b, j: (b, KI_COL_BLOCK)),
            pl.BlockSpec((nkt, ATT_HEADS * VT_ROWS, TK), lambda b, j: (b, 0, 0)),
            pl.BlockSpec((WI_ROWS, QB), lambda b, j: (0, b * nq + j)),
        ],
        out_specs=pl.BlockSpec((QB, ATT_WIDTH), lambda b, j: (b * nq + j, 0)),
        out_shape=jax.ShapeDtypeStruct((t, ATT_WIDTH), MXU_DTYPE),
        scratch_shapes=[
            pltpu.VMEM((ATT_HEADS, LANES, QB), MXU_DTYPE),
            pltpu.VMEM((IDX_HEADS, LANES, QB), MXU_DTYPE),
            pltpu.VMEM((TK, TK), MXU_DTYPE),
            pltpu.VMEM(((nkt + 1) // 2, 2 * TK, QB), F32),
            pltpu.VMEM((ATT_HEADS, TK, QB), F32),
            pltpu.VMEM((ATT_HEADS, TK, QB), MXU_DTYPE),
            pltpu.VMEM((ATT_WIDTH, QB), F32),
            pltpu.VMEM((ATT_HEADS, QB), F32),
            pltpu.VMEM((ATT_HEADS, QB), F32),
        ],
        compiler_params=pltpu.CompilerParams(
            dimension_semantics=("parallel", "parallel"), vmem_limit_bytes=VMEM_LIMIT),
        name="attn",
    )(qt, qt, o16, o16, vt, wit)


def _mix_kernel(x_ref, ya_ref, cur_ref, halo_ref, wp_ref, ps_ref, dw_ref, dwb_ref, lng_ref, lnb_ref,
                pw_ref, pwb_ref, wo_ref, o_ref, ubuf, hbuf, hsh, *, tm):
    j = pl.program_id(1)
    cur = cur_ref[...]
    halo = jnp.where(j > 0, halo_ref[...], 0.0)

    def glu(z):
        return z[:, POOL_WIDTH:POOL_WIDTH + CONV_WIDTH] * jax.nn.sigmoid(z[:, POOL_WIDTH + CONV_WIDTH:])

    u = cur[:, :POOL_WIDTH]
    ubuf[0:HALO, :] = halo[:, :POOL_WIDTH]
    ubuf[HALO:, :] = u
    hbuf[0:HALO, :] = glu(halo)
    hbuf[HALO:, :] = glu(cur)

    lane = lax.broadcasted_iota(jnp.int32, (tm, LANES), 1)
    upper = lane >= POOL_GROUP_DIM
    s0 = u[:, :LANES]
    s1 = u[:, LANES:]
    for i in range(1, POOL_WINDOWS[3]):
        if i < POOL_WINDOWS[1]:
            sh = ubuf[HALO - i:HALO - i + tm, 0:LANES]
            s0 = s0 + (sh if i < POOL_WINDOWS[0] else jnp.where(upper, sh, 0.0))
        sh = ubuf[HALO - i:HALO - i + tm, LANES:2 * LANES]
        s1 = s1 + (sh if i < POOL_WINDOWS[2] else jnp.where(upper, sh, 0.0))
    t1 = (j * tm + lax.broadcasted_iota(jnp.int32, (tm, LANES), 0) + 1).astype(F32)
    w0 = jnp.where(upper, float(POOL_WINDOWS[1]), float(POOL_WINDOWS[0]))
    w1 = jnp.where(upper, float(POOL_WINDOWS[3]), float(POOL_WINDOWS[2]))
    pooled = jnp.concatenate([s0 / jnp.minimum(t1, w0), s1 / jnp.minimum(t1, w1)], axis=1)
    d = (pooled - u).astype(MXU_DTYPE)
    yb = jnp.dot(d, wp_ref[...], preferred_element_type=F32) * ps_ref[...]

    span = tm + HALO - SUBLANES
    for ph in range(1, SUBLANES):
        hsh[ph - 1] = hbuf[ph:ph + span, :]
    c = jnp.zeros((tm, CONV_WIDTH), F32) + dwb_ref[...]
    off = HALO - (CONV_KERNEL - 1)
    for jj in range(CONV_KERNEL):
        a, ph = divmod(off + jj, SUBLANES)
        rows = slice(a * SUBLANES, a * SUBLANES + tm)
        tap = hbuf[rows, :] if ph == 0 else hsh[ph - 1, rows, :]
        c = c + tap * dw_ref[jj:jj + 1, :]
    mu = jnp.mean(c, axis=-1, keepdims=True)
    cc = c - mu
    var = jnp.mean(cc * cc, axis=-1, keepdims=True)
    hn = cc * lax.rsqrt(var + EPS) * lng_ref[...] + lnb_ref[...]
    sw = (hn * jax.nn.sigmoid(hn)).astype(MXU_DTYPE)
    yc = jnp.dot(sw, pw_ref[...], preferred_element_type=F32) + pwb_ref[...]

    y = jnp.dot(ya_ref[...], wo_ref[0:ATT_WIDTH, :], preferred_element_type=F32)
    y = y + jnp.dot(yb.astype(MXU_DTYPE), wo_ref[ATT_WIDTH:ATT_WIDTH + POOL_WIDTH, :], preferred_element_type=F32)
    y = y + jnp.dot(yc.astype(MXU_DTYPE), wo_ref[ATT_WIDTH + POOL_WIDTH:, :], preferred_element_type=F32)
    o_ref[...] = x_ref[...] + y


def _mix(x2d, ya, o32, wp, ps, dw, dwb, lng, lnb, pw, pwb, wo, batch, seq, tm):
    t = batch * seq
    nt = seq // tm
    hb = tm // HALO
    full = lambda b, j: (0, 0)
    return pl.pallas_call(
        functools.partial(_mix_kernel, tm=tm),
        grid=(batch, nt),
        in_specs=[
            pl.BlockSpec((tm, D_MODEL), lambda b, j: (b * nt + j, 0)),
            pl.BlockSpec((tm, ATT_WIDTH), lambda b, j: (b * nt + j, 0)),
            pl.BlockSpec((tm, N32), lambda b, j: (b * nt + j, 0)),
            pl.BlockSpec((HALO, N32), lambda b, j: (jnp.maximum((b * nt + j) * hb - 1, 0), 0)),
            pl.BlockSpec((POOL_WIDTH, POOL_WIDTH), full),
            pl.BlockSpec((1, POOL_WIDTH), full),
            pl.BlockSpec((HALO, CONV_WIDTH), full),
            pl.BlockSpec((1, CONV_WIDTH), full),
            pl.BlockSpec((1, CONV_WIDTH), full),
            pl.BlockSpec((1, CONV_WIDTH), full),
            pl.BlockSpec((CONV_WIDTH, CONV_WIDTH), full),
            pl.BlockSpec((1, CONV_WIDTH), full),
            pl.BlockSpec((D_MODEL, D_MODEL), full),
        ],
        out_specs=pl.BlockSpec((tm, D_MODEL), lambda b, j: (b * nt + j, 0)),
        out_shape=jax.ShapeDtypeStruct((t, D_MODEL), F32),
        scratch_shapes=[
            pltpu.VMEM((HALO + tm, POOL_WIDTH), F32),
            pltpu.VMEM((HALO + tm, CONV_WIDTH), F32),
            pltpu.VMEM((SUBLANES - 1, HALO + tm - SUBLANES, CONV_WIDTH), F32),
        ],
        compiler_params=pltpu.CompilerParams(
            dimension_semantics=("parallel", "parallel"), vmem_limit_bytes=VMEM_LIMIT),
        name="mix",
    )(x2d, ya, o32, o32, wp, ps, dw, dwb, lng, lnb, pw, pwb, wo)


def _route(glog, elog):
    lane = lax.broadcasted_iota(jnp.int32, glog.shape, 1)
    lane_f = lane.astype(F32)
    big = float(LANES)
    gl = jnp.where(lane < N_GROUPS, glog, NEG_INF)
    ge = jnp.exp(gl - jnp.max(gl, axis=-1, keepdims=True))
    gp = ge / jnp.sum(ge, axis=-1, keepdims=True)
    p_g = jnp.max(gp, axis=-1, keepdims=True)
    g_sel = jnp.min(jnp.where(gp == p_g, lane_f, big), axis=-1, keepdims=True)
    in_grp = (lane // EXPERTS_PER_GROUP).astype(F32) == g_sel
    el = jnp.where(in_grp, elog, NEG_INF)
    ee = jnp.exp(el - jnp.max(el, axis=-1, keepdims=True))
    ep = ee / jnp.sum(ee, axis=-1, keepdims=True)
    ep = jnp.where(in_grp, ep, -1.0)
    v1 = jnp.max(ep, axis=-1, keepdims=True)
    i1 = jnp.min(jnp.where(ep == v1, lane_f, big), axis=-1, keepdims=True)
    ep2 = jnp.where(lane_f == i1, -1.0, ep)
    v2 = jnp.max(ep2, axis=-1, keepdims=True)
    i2 = jnp.min(jnp.where(ep2 == v2, lane_f, big), axis=-1, keepdims=True)
    den = v1 + v2
    w_e = jnp.where(lane_f == i1, v1 / den, jnp.where(lane_f == i2, v2 / den, 0.0))
    return p_g * w_e, g_sel


def _moe_kernel(x_ref, g2_ref, rgw_ref, rgb_ref, rew_ref, reb_ref, tri_ref, wg_ref, wu_ref, wd_ref, fg_ref,
                o_ref, h_ref, gate3_ref, info_ref, infot_ref, xg_ref, gg_ref, yg_ref, cnt_ref,
                *, tm, final_norm):
    e = pl.program_id(1)
    grp = e // EXPERTS_PER_GROUP
    le = e % EXPERTS_PER_GROUP
    grp_f = grp.astype(F32)

    @pl.when(e == 0)
    def _():
        x = x_ref[...]
        hb = _rms(x, g2_ref[...]).astype(MXU_DTYPE)
        h_ref[...] = hb
        glog = jnp.dot(hb, rgw_ref[...], preferred_element_type=F32) + rgb_ref[...]
        elog = jnp.dot(hb, rew_ref[...], preferred_element_type=F32) + reb_ref[...]
        gate, g_sel = _route(glog, elog)
        g1 = gate.astype(MXU_DTYPE)
        r1 = gate - g1.astype(F32)
        g2 = r1.astype(MXU_DTYPE)
        gate3_ref[0] = g1
        gate3_ref[1] = g2
        gate3_ref[2] = (r1 - g2.astype(F32)).astype(MXU_DTYPE)
        lane = lax.broadcasted_iota(jnp.int32, (tm, LANES), 1)
        member = jnp.where(lane.astype(F32) == g_sel, 1.0, 0.0).astype(MXU_DTYPE)
        ranks = jnp.dot(tri_ref[...], member, preferred_element_type=F32)
        info = jnp.where(lane < N_GROUPS, ranks, jnp.where(lane == N_GROUPS, g_sel, 0.0))
        info_ref[...] = info
        infot_ref[...] = info.T[:8, :]
        for gi in range(N_GROUPS):
            cnt_ref[gi] = ranks[tm - 1, gi].astype(jnp.int32)
        o_ref[...] = x

    n_rows = cnt_ref[grp]
    n_blk = (n_rows + (RB - 1)) // RB

    @pl.when(le == 0)
    def _():
        rank_t = infot_ref[pl.ds(grp, 1), :]
        member_t = infot_ref[N_GROUPS:N_GROUPS + 1, :] == grp_f

        def gather(rb, carry):
            r0 = pl.multiple_of(rb * RB, RB)
            want = (r0 + 1 + lax.broadcasted_iota(jnp.int32, (RB, tm), 0)).astype(F32)
            pick = jnp.where(member_t & (rank_t == want), 1.0, 0.0).astype(MXU_DTYPE)
            xg_ref[pl.ds(r0, RB), :] = jnp.dot(pick, h_ref[...], preferred_element_type=F32).astype(xg_ref.dtype)
            gg_ref[pl.ds(r0, RB), :] = (jnp.dot(pick, gate3_ref[0], preferred_element_type=F32)
                                        + jnp.dot(pick, gate3_ref[1], preferred_element_type=F32)
                                        + jnp.dot(pick, gate3_ref[2], preferred_element_type=F32))
            return carry

        lax.fori_loop(0, n_blk, gather, 0)
        yg_ref[...] = jnp.zeros(yg_ref.shape, F32)

    wg = wg_ref[0]
    wu = wu_ref[0]
    wd = wd_ref[0]

    def expert(rb, carry):
        r0 = pl.multiple_of(rb * RB, RB)
        xb = xg_ref[pl.ds(r0, RB), :]
        a = jnp.dot(xb, wg, preferred_element_type=F32)
        b = jnp.dot(xb, wu, preferred_element_type=F32)
        lane = lax.broadcasted_iota(jnp.int32, (RB, LANES), 1)
        g_col = jnp.sum(jnp.where(lane == e, gg_ref[pl.ds(r0, RB), :], 0.0), axis=-1, keepdims=True)
        act = (a * jax.nn.sigmoid(a)) * b * g_col
        yg_ref[pl.ds(r0, RB), :] += jnp.dot(act.astype(MXU_DTYPE), wd, preferred_element_type=F32)
        return carry

    lax.fori_loop(0, n_blk, expert, 0)

    @pl.when(le == EXPERTS_PER_GROUP - 1)
    def _():
        lane = lax.broadcasted_iota(jnp.int32, (tm, LANES), 1)
        info = info_ref[...]
        rank_c = jnp.sum(jnp.where(lane == grp, info, 0.0), axis=-1, keepdims=True)
        member_c = jnp.sum(jnp.where(lane == N_GROUPS, info, 0.0), axis=-1, keepdims=True) == grp_f

        def scatter(sb, carry):
            r0 = pl.multiple_of(sb * SB, SB)
            want = (r0 + 1 + lax.broadcasted_iota(jnp.int32, (tm, SB), 1)).astype(F32)
            put = jnp.where(member_c & (rank_c == want), 1.0, 0.0).astype(MXU_DTYPE)
            y = yg_ref[pl.ds(r0, SB), :]
            y_hi = y.astype(MXU_DTYPE)
            y_lo = (y - y_hi.astype(F32)).astype(MXU_DTYPE)
            o_ref[...] += (jnp.dot(put, y_hi, preferred_element_type=F32)
                           + jnp.dot(put, y_lo, preferred_element_type=F32))
            return carry

        lax.fori_loop(0, (n_rows + (SB - 1)) // SB, scatter, 0)

    if final_norm:
        @pl.when(e == N_EXPERTS - 1)
        def _():
            o_ref[...] = _rms(o_ref[...], fg_ref[...])


def _moe(x2d, g2, rgw, rgb, rew, reb, tri, wg, wu, wd, fg, layer, final_norm, tm):
    t = x2d.shape[0]
    cap = pl.cdiv(tm, RB) * RB
    full = lambda i, e: (0, 0)
    return pl.pallas_call(
        functools.partial(_moe_kernel, tm=tm, final_norm=final_norm),
        grid=(t // tm, N_EXPERTS),
        in_specs=[
            pl.BlockSpec((tm, D_MODEL), lambda i, e: (i, 0)),
            pl.BlockSpec((1, D_MODEL), full),
            pl.BlockSpec((D_MODEL, LANES), full),
            pl.BlockSpec((1, LANES), full),
            pl.BlockSpec((D_MODEL, LANES), full),
            pl.BlockSpec((1, LANES), full),
            pl.BlockSpec((tm, tm), full),
            pl.BlockSpec((1, D_MODEL, EXPERT_HIDDEN), lambda i, e: (layer * N_EXPERTS + e, 0, 0)),
            pl.BlockSpec((1, D_MODEL, EXPERT_HIDDEN), lambda i, e: (layer * N_EXPERTS + e, 0, 0)),
            pl.BlockSpec((1, EXPERT_HIDDEN, D_MODEL), lambda i, e: (layer * N_EXPERTS + e, 0, 0)),
            pl.BlockSpec((1, D_MODEL), full),
        ],
        out_specs=pl.BlockSpec((tm, D_MODEL), lambda i, e: (i, 0)),
        out_shape=jax.ShapeDtypeStruct((t, D_MODEL), F32),
        scratch_shapes=[
            pltpu.VMEM((tm, D_MODEL), MXU_DTYPE),
            pltpu.VMEM((3, tm, LANES), MXU_DTYPE),
            pltpu.VMEM((tm, LANES), F32),
            pltpu.VMEM((8, tm), F32),
            pltpu.VMEM((cap, D_MODEL), MXU_DTYPE),
            pltpu.VMEM((cap, LANES), F32),
            pltpu.VMEM((cap, D_MODEL), F32),
            pltpu.SMEM((N_GROUPS,), jnp.int32),
        ],
        compiler_params=pltpu.CompilerParams(
            dimension_semantics=("parallel", "arbitrary"), vmem_limit_bytes=MOE_VMEM_LIMIT),
        name="moe",
    )(x2d, g2, rgw, rgb, rew, reb, tri, wg, wu, wd, fg)


def _pad_lanes(w):
    return jnp.pad(w, ((0, 0), (0, LANES - w.shape[-1])))


def _block_diag(blocks):
    g, n, _ = blocks.shape
    out = jnp.zeros((g * n, g * n), blocks.dtype)
    for i in range(g):
        out = out.at[i * n:(i + 1) * n, i * n:(i + 1) * n].set(blocks[i])
    return out


def kernel(x, norm1_g, w_in, pool_w, pool_scale, dw_w, dw_b, conv_ln_g, conv_ln_b, pw_w, pw_b,
           w_out, norm2_g, rg_w, rg_b, re_w, re_b, w_gate, w_up, w_down, final_g):
    batch, seq, d = x.shape
    depth = w_in.shape[0]
    t = batch * seq
    tm = min(512, seq)
    tm_moe = min(1024, t)
    assert d == D_MODEL and seq % QB == 0 and seq % tm == 0 and tm % TK == 0 and t % tm_moe == 0

    o_q, o_k, o_v = 0, ATT_WIDTH, 2 * ATT_WIDTH
    o_qi = 3 * ATT_WIDTH
    o_ki = o_qi + IDX_HEADS * IDX_HEAD_DIM
    o_wi = o_ki + IDX_HEAD_DIM
    o_pool = o_wi + IDX_HEADS
    o_conv = o_pool + POOL_WIDTH

    wg = w_gate.reshape(depth * N_EXPERTS, D_MODEL, EXPERT_HIDDEN).astype(MXU_DTYPE)
    wu = w_up.reshape(depth * N_EXPERTS, D_MODEL, EXPERT_HIDDEN).astype(MXU_DTYPE)
    wd = w_down.reshape(depth * N_EXPERTS, EXPERT_HIDDEN, D_MODEL).astype(MXU_DTYPE)
    fg = final_g.reshape(1, D_MODEL)
    tri = jnp.tri(tm_moe, dtype=MXU_DTYPE)

    xf = x.reshape(t, D_MODEL)
    for l in range(depth):
        w = w_in[l]
        w_ki = w[:, o_ki:o_wi]
        w_cat = jnp.concatenate([w[:, o_k:o_v], w_ki, w_ki, w[:, o_pool:]], axis=1).astype(MXU_DTYPE)
        wt = jnp.concatenate([w[:, o_q:o_k], w[:, o_qi:o_ki], w[:, o_v:o_qi]], axis=1).T.astype(MXU_DTYPE)
        wwit = jnp.pad(w[:, o_wi:o_pool].T, ((0, WI_ROWS - IDX_HEADS), (0, 0))).astype(MXU_DTYPE)
        o16, o32, qt, vt, wit = _proj(xf, norm1_g[l].reshape(1, D_MODEL), w_cat, wt, wwit, tm)

        ya = _attn(qt, o16, vt, wit, batch, seq)

        x1 = _mix(
            xf, ya, o32,
            _block_diag(pool_w[l]).astype(MXU_DTYPE), pool_scale[l].reshape(1, POOL_WIDTH),
            jnp.pad(dw_w[l], ((0, HALO - CONV_KERNEL), (0, 0))), dw_b[l].reshape(1, CONV_WIDTH),
            conv_ln_g[l].reshape(1, CONV_WIDTH), conv_ln_b[l].reshape(1, CONV_WIDTH),
            pw_w[l].astype(MXU_DTYPE), pw_b[l].reshape(1, CONV_WIDTH),
            w_out[l].astype(MXU_DTYPE), batch, seq, tm)

        xf = _moe(
            x1, norm2_g[l].reshape(1, D_MODEL),
            _pad_lanes(rg_w[l]).astype(MXU_DTYPE), _pad_lanes(rg_b[l].reshape(1, N_GROUPS)),
            _pad_lanes(re_w[l]).astype(MXU_DTYPE), _pad_lanes(re_b[l].reshape(1, N_EXPERTS)),
            tri, wg, wu, wd, fg, l, l == depth - 1, tm_moe)
    return xf.reshape(batch, seq, D_MODEL)
```

```python
import functools

import jax
import jax.numpy as jnp
from jax import lax
from jax.experimental import pallas as pl
from jax.experimental.pallas import tpu as pltpu

F32 = jnp.float32
MXU_DTYPE = jnp.bfloat16
COARSE_DTYPE = jnp.bfloat16

D_MODEL = 1024
CHUNK = 64
ATT_HEADS = 8
ATT_HEAD_DIM = 64
ATT_WIDTH = ATT_HEADS * ATT_HEAD_DIM
IDX_HEADS = 8
IDX_HEAD_DIM = 64
TOPK_MAX = 256
POOL_GROUPS = 4
POOL_GROUP_DIM = 64
POOL_WIDTH = POOL_GROUPS * POOL_GROUP_DIM
POOL_WINDOWS = (2, 4, 8, 16)
CONV_WIDTH = 256
CONV_KERNEL = 31
N_GROUPS = 4
EXPERTS_PER_GROUP = 4
N_EXPERTS = N_GROUPS * EXPERTS_PER_GROUP
EXPERT_HIDDEN = 512
EPS = 1e-6

LANES = 128
SUBLANES = 8
INT_MIN = -2 ** 31
NEG_INF = float("-inf")

N16 = ATT_WIDTH + 2 * IDX_HEAD_DIM
KI_COL_BLOCK = ATT_WIDTH // LANES
N32 = POOL_WIDTH + 2 * CONV_WIDTH
NT_ROWS = 3 * ATT_WIDTH
WI_ROWS = 16
VT_ROWS = ATT_HEAD_DIM + 16
KEY_NEG_INF = (0xFF800000 ^ 0x7FFFFFFF) - 2 ** 32

QB = 512
TK = 256
HEAD_GROUP = 8
SEARCH_COLS = 256
CNT16_ROWS = 32
FINE_BITS = 17
CNT_ROWS = 16
HALO = 32
VMEM_LIMIT = 48 * 1024 * 1024
MOE_VMEM_LIMIT = 56 * 1024 * 1024
RB = 288
SB = 256

_NT = (((1,), (1,)), ((), ()))


def _rms(x, g):
    return x * lax.rsqrt(jnp.mean(x * x, axis=-1, keepdims=True) + EPS) * g


def _proj_kernel(x_ref, g_ref, w_ref, wt_ref, wwit_ref, o16_ref, o32_ref, qt_ref, vt_ref, wit_ref, *, tm):
    h = _rms(x_ref[...], g_ref[...]).astype(MXU_DTYPE)
    p = jnp.dot(h, w_ref[...], preferred_element_type=F32)
    o16_ref[...] = p[:, :N16].astype(o16_ref.dtype)
    o32_ref[...] = p[:, N16:]
    pt = lax.dot_general(wt_ref[...], h, _NT, preferred_element_type=F32)
    qt_ref[...] = pt[:2 * ATT_WIDTH, :].astype(qt_ref.dtype)
    ones = jnp.ones((VT_ROWS - ATT_HEAD_DIM, TK), vt_ref.dtype)
    for c in range(tm // TK):
        for hd in range(ATT_HEADS):
            r0 = 2 * ATT_WIDTH + hd * ATT_HEAD_DIM
            vt_ref[c, hd * VT_ROWS:hd * VT_ROWS + ATT_HEAD_DIM, :] = (
                pt[r0:r0 + ATT_HEAD_DIM, c * TK:(c + 1) * TK].astype(vt_ref.dtype))
            vt_ref[c, hd * VT_ROWS + ATT_HEAD_DIM:(hd + 1) * VT_ROWS, :] = ones
    wit_ref[...] = lax.dot_general(wwit_ref[...], h, _NT, preferred_element_type=F32)


def _proj(x2d, g, w, wt, wwit, tm):
    t = x2d.shape[0]
    return pl.pallas_call(
        functools.partial(_proj_kernel, tm=tm),
        grid=(t // tm,),
        in_specs=[
            pl.BlockSpec((tm, D_MODEL), lambda i: (i, 0)),
            pl.BlockSpec((1, D_MODEL), lambda i: (0, 0)),
            pl.BlockSpec((D_MODEL, N16 + N32), lambda i: (0, 0)),
            pl.BlockSpec((NT_ROWS, D_MODEL), lambda i: (0, 0)),
            pl.BlockSpec((WI_ROWS, D_MODEL), lambda i: (0, 0)),
        ],
        out_specs=[
            pl.BlockSpec((tm, N16), lambda i: (i, 0)),
            pl.BlockSpec((tm, N32), lambda i: (i, 0)),
            pl.BlockSpec((2 * ATT_WIDTH, tm), lambda i: (0, i)),
            pl.BlockSpec((tm // TK, ATT_HEADS * VT_ROWS, TK), lambda i: (i, 0, 0)),
            pl.BlockSpec((WI_ROWS, tm), lambda i: (0, i)),
        ],
        out_shape=[
            jax.ShapeDtypeStruct((t, N16), MXU_DTYPE),
            jax.ShapeDtypeStruct((t, N32), F32),
            jax.ShapeDtypeStruct((2 * ATT_WIDTH, t), MXU_DTYPE),
            jax.ShapeDtypeStruct((t // TK, ATT_HEADS * VT_ROWS, TK), MXU_DTYPE),
            jax.ShapeDtypeStruct((WI_ROWS, t), F32),
        ],
        compiler_params=pltpu.CompilerParams(
            dimension_semantics=("parallel",), vmem_limit_bytes=VMEM_LIMIT),
        name="proj",
    )(x2d, g, w, wt, wwit)


def _attn_kernel(qt_ref, qit_ref, k_ref, ki_ref, vt_ref, wit_ref, o_ref,
                 qm_ref, qim_ref, tri_ref, sc_ref, sc16_ref, lg_ref, p_ref, acc_ref, m_ref, l_ref, *, ktop):
    j = pl.program_id(1)
    nkt = (j + 1) * (QB // TK)

    row = lax.broadcasted_iota(jnp.int32, (LANES, QB), 0)
    for h in range(ATT_HEADS):
        pr, half = divmod(h, 2)
        keep = (row < ATT_HEAD_DIM) if half == 0 else (row >= ATT_HEAD_DIM)
        qp = qt_ref[pr * LANES:(pr + 1) * LANES, :].astype(F32) * (ATT_HEAD_DIM ** -0.5)
        qm_ref[h] = jnp.where(keep, qp, 0.0).astype(qm_ref.dtype)
        qip = qit_ref[pr * LANES:(pr + 1) * LANES, :].astype(F32) * (IDX_HEAD_DIM ** -0.5)
        qim_ref[h] = jnp.where(keep, qip, 0.0).astype(qim_ref.dtype)

    r_i = lax.broadcasted_iota(jnp.int32, (TK, TK), 0)
    c_i = lax.broadcasted_iota(jnp.int32, (TK, TK), 1)
    tri_ref[...] = jnp.where(c_i <= r_i, 1.0, 0.0).astype(tri_ref.dtype)

    wt = wit_ref[...]
    q_chunk = (j * QB + lax.broadcasted_iota(jnp.int32, (1, QB), 1)) // CHUNK

    def admissible(k0):
        k_chunk = (k0 + lax.broadcasted_iota(jnp.int32, (TK, 1), 0)) // CHUNK
        return k_chunk <= q_chunk

    def score_tile(kt, carry):
        k0 = pl.multiple_of(kt * TK, TK)
        kit = ki_ref[pl.ds(k0, TK), :]
        acc = jnp.zeros((TK, QB), F32)
        for h in range(IDX_HEADS):
            d = jnp.dot(kit, qim_ref[h], preferred_element_type=F32)
            acc = acc + jnp.maximum(d, 0.0) * wt[h:h + 1, :]
        score = jnp.where(admissible(k0), acc * (IDX_HEADS ** -0.5), NEG_INF)
        half = pl.ds(pl.multiple_of((kt % 2) * TK, TK), TK)
        sc_ref[kt // 2, half, :] = score
        sc16_ref[kt // 2, half, :] = score.astype(sc16_ref.dtype)
        return carry

    lax.fori_loop(0, nkt, score_tile, 0)

    @pl.when(nkt % 2 == 1)
    def _():
        sc_ref[nkt // 2, TK:, :] = jnp.full((TK, QB), NEG_INF, F32)
        sc16_ref[nkt // 2, TK:, :] = jnp.full((TK, QB), NEG_INF, sc16_ref.dtype)

    n_slab = (nkt + 1) // 2

    def key_to_float(key):
        bits = key ^ ((key >> 31) & 0x7FFFFFFF)
        return jnp.where(key < KEY_NEG_INF, NEG_INF, lax.bitcast_convert_type(bits, F32))

    def coarse_key(u):
        key = lax.shift_left(u, 16) ^ INT_MIN
        return jnp.where(key < 0, key | 0xFFFF, key)

    def count(pred_fn, c0):
        def body(kp, cnt):
            ones = jnp.where(pred_fn(sc_ref[kp, :, c0:c0 + SEARCH_COLS]), 1.0, 0.0)
            return cnt + jnp.sum(ones.reshape(2 * TK // CNT_ROWS, CNT_ROWS, SEARCH_COLS), axis=0)
        cnt = lax.fori_loop(0, n_slab, body, jnp.zeros((CNT_ROWS, SEARCH_COLS), F32))
        return jnp.sum(cnt, axis=0, keepdims=True)

    def count16(cand, c0):
        one = jnp.ones((), sc16_ref.dtype)
        zero = jnp.zeros((), sc16_ref.dtype)

        def body(kp, cnt):
            ones = jnp.where(sc16_ref[kp, :, c0:c0 + SEARCH_COLS] >= cand, one, zero)
            parts = [ones[r:r + CNT16_ROWS, :] for r in range(0, 2 * TK, CNT16_ROWS)]
            while len(parts) > 1:
                parts = [a + b for a, b in zip(parts[0::2], parts[1::2])]
            return cnt + parts[0].astype(F32)
        cnt = lax.fori_loop(0, n_slab, body, jnp.zeros((CNT16_ROWS, SEARCH_COLS), F32))
        return jnp.sum(cnt, axis=0, keepdims=True)

    thr_parts, need_parts = [], []
    for c0 in range(0, QB, SEARCH_COLS):
        def coarse_step(i, prefix, c0=c0):
            cand_u = prefix | lax.shift_left(jnp.int32(1), 15 - i)
            cand = key_to_float(coarse_key(cand_u)).astype(sc16_ref.dtype)
            return jnp.where(count16(cand, c0) >= ktop, cand_u, prefix)

        lead = lax.fori_loop(0, 16, coarse_step, jnp.zeros((1, SEARCH_COLS), jnp.int32))
        base = jnp.maximum(coarse_key(lead) - (1 << 15), KEY_NEG_INF)

        def fine_step(i, off, c0=c0, base=base):
            cand_off = off | lax.shift_left(jnp.int32(1), FINE_BITS - 1 - i)
            cand = key_to_float(base + cand_off)
            return jnp.where(count(lambda s: s >= cand, c0) >= ktop, cand_off, off)

        off = lax.fori_loop(0, FINE_BITS, fine_step, jnp.zeros((1, SEARCH_COLS), jnp.int32))
        thr_c = key_to_float(base + off)
        thr_parts.append(thr_c)
        need_parts.append(ktop - count(lambda s: s > thr_c, c0))
    thr = jnp.concatenate(thr_parts, axis=1)
    need = jnp.concatenate(need_parts, axis=1)

    m_ref[...] = jnp.full(m_ref.shape, NEG_INF, F32)
    l_ref[...] = jnp.zeros(l_ref.shape, F32)
    acc_ref[...] = jnp.zeros(acc_ref.shape, F32)

    def attend_tile(kt, eq_before):
        k0 = pl.multiple_of(kt * TK, TK)
        s = sc_ref[kt // 2, pl.ds(pl.multiple_of((kt % 2) * TK, TK), TK), :]
        eq = s == thr
        eqf = jnp.where(eq, 1.0, 0.0)
        incl = jnp.dot(tri_ref[...], eqf.astype(tri_ref.dtype), preferred_element_type=F32)
        sel = ((s > thr) | (eq & ((eq_before + incl) <= need))) & admissible(k0)
        bias = jnp.where(sel, 0.0, NEG_INF)
        for h0 in range(0, ATT_HEADS, HEAD_GROUP):
            heads = range(h0, h0 + HEAD_GROUP)
            alphas = {}
            for h in heads:
                pr = h // 2
                kp = k_ref[pl.ds(k0, TK), pr * LANES:(pr + 1) * LANES]
                lg = jnp.dot(kp, qm_ref[h], preferred_element_type=F32) + bias
                lg_ref[h] = lg
                m_old = m_ref[h:h + 1, :]
                m_tile = jnp.max(lg.reshape(TK // CNT_ROWS, CNT_ROWS, QB), axis=0)
                m_new = jnp.maximum(m_old, jnp.max(m_tile, axis=0, keepdims=True))
                m_ref[h:h + 1, :] = m_new
                m_safe = jnp.where(m_new == NEG_INF, 0.0, m_new)
                alphas[h] = (jnp.exp(m_old - m_safe), m_safe)
            for h in heads:
                p_ref[h] = jnp.exp(lg_ref[h] - alphas[h][1]).astype(p_ref.dtype)
            for h in heads:
                alpha = alphas[h][0]
                pv = jnp.dot(vt_ref[kt, h * VT_ROWS:(h + 1) * VT_ROWS, :], p_ref[h],
                             preferred_element_type=F32)
                rows = slice(h * ATT_HEAD_DIM, (h + 1) * ATT_HEAD_DIM)
                acc_ref[rows, :] = alpha * acc_ref[rows, :] + pv[:ATT_HEAD_DIM, :]
                l_ref[h:h + 1, :] = alpha * l_ref[h:h + 1, :] + pv[ATT_HEAD_DIM:ATT_HEAD_DIM + 1, :]
        return eq_before + jnp.sum(eqf, axis=0, keepdims=True)

    lax.fori_loop(0, nkt, attend_tile, jnp.zeros((1, QB), F32))

    for h in range(ATT_HEADS):
        rows = slice(h * ATT_HEAD_DIM, (h + 1) * ATT_HEAD_DIM)
        acc_ref[rows, :] = acc_ref[rows, :] / l_ref[h:h + 1, :]
    o_ref[...] = acc_ref[...].T.astype(o_ref.dtype)


def _attn(qt, o16, vt, wit, batch, seq):
    t = batch * seq
    nq = seq // QB
    nkt = seq // TK
    ktop = min(TOPK_MAX, seq // 4)
    return pl.pallas_call(
        functools.partial(_attn_kernel, ktop=ktop),
        grid=(batch, nq),
        in_specs=[
            pl.BlockSpec((ATT_WIDTH, QB), lambda b, j: (0, b * nq + j)),
            pl.BlockSpec((ATT_WIDTH, QB), lambda b, j: (1, b * nq + j)),
            pl.BlockSpec((seq, ATT_WIDTH), lambda b, j: (b, 0)),
            pl.BlockSpec((seq, LANES), lambda b, j: (b, KI_COL_BLOCK)),
            pl.BlockSpec((nkt, ATT_HEADS * VT_ROWS, TK), lambda b, j: (b, 0, 0)),
            pl.BlockSpec((WI_ROWS, QB), lambda b, j: (0, b * nq + j)),
        ],
        out_specs=pl.BlockSpec((QB, ATT_WIDTH), lambda b, j: (b * nq + j, 0)),
        out_shape=jax.ShapeDtypeStruct((t, ATT_WIDTH), MXU_DTYPE),
        scratch_shapes=[
            pltpu.VMEM((ATT_HEADS, LANES, QB), MXU_DTYPE),
            pltpu.VMEM((IDX_HEADS, LANES, QB), MXU_DTYPE),
            pltpu.VMEM((TK, TK), MXU_DTYPE),
            pltpu.VMEM(((nkt + 1) // 2, 2 * TK, QB), F32),
            pltpu.VMEM(((nkt + 1) // 2, 2 * TK, QB), COARSE_DTYPE),
            pltpu.VMEM((ATT_HEADS, TK, QB), F32),
            pltpu.VMEM((ATT_HEADS, TK, QB), MXU_DTYPE),
            pltpu.VMEM((ATT_WIDTH, QB), F32),
            pltpu.VMEM((ATT_HEADS, QB), F32),
            pltpu.VMEM((ATT_HEADS, QB), F32),
        ],
        compiler_params=pltpu.CompilerParams(
            dimension_semantics=("parallel", "parallel"), vmem_limit_bytes=VMEM_LIMIT),
        name="attn",
    )(qt, qt, o16, o16, vt, wit)


def _mix_kernel(x_ref, ya_ref, cur_ref, halo_ref, wp_ref, ps_ref, dw_ref, dwb_ref, lng_ref, lnb_ref,
                pw_ref, pwb_ref, wo_ref, o_ref, ubuf, hbuf, hsh, *, tm):
    j = pl.program_id(1)
    cur = cur_ref[...]
    halo = jnp.where(j > 0, halo_ref[...], 0.0)

    def glu(z):
        return z[:, POOL_WIDTH:POOL_WIDTH + CONV_WIDTH] * jax.nn.sigmoid(z[:, POOL_WIDTH + CONV_WIDTH:])

    u = cur[:, :POOL_WIDTH]
    ubuf[0:HALO, :] = halo[:, :POOL_WIDTH]
    ubuf[HALO:, :] = u
    hbuf[0:HALO, :] = glu(halo)
    hbuf[HALO:, :] = glu(cur)

    lane = lax.broadcasted_iota(jnp.int32, (tm, LANES), 1)
    upper = lane >= POOL_GROUP_DIM
    s0 = u[:, :LANES]
    s1 = u[:, LANES:]
    for i in range(1, POOL_WINDOWS[3]):
        if i < POOL_WINDOWS[1]:
            sh = ubuf[HALO - i:HALO - i + tm, 0:LANES]
            s0 = s0 + (sh if i < POOL_WINDOWS[0] else jnp.where(upper, sh, 0.0))
        sh = ubuf[HALO - i:HALO - i + tm, LANES:2 * LANES]
        s1 = s1 + (sh if i < POOL_WINDOWS[2] else jnp.where(upper, sh, 0.0))
    t1 = (j * tm + lax.broadcasted_iota(jnp.int32, (tm, LANES), 0) + 1).astype(F32)
    w0 = jnp.where(upper, float(POOL_WINDOWS[1]), float(POOL_WINDOWS[0]))
    w1 = jnp.where(upper, float(POOL_WINDOWS[3]), float(POOL_WINDOWS[2]))
    pooled = jnp.concatenate([s0 / jnp.minimum(t1, w0), s1 / jnp.minimum(t1, w1)], axis=1)
    d = (pooled - u).astype(MXU_DTYPE)
    yb = jnp.dot(d, wp_ref[...], preferred_element_type=F32) * ps_ref[...]

    span = tm + HALO - SUBLANES
    for ph in range(1, SUBLANES):
        hsh[ph - 1] = hbuf[ph:ph + span, :]
    c = jnp.zeros((tm, CONV_WIDTH), F32) + dwb_ref[...]
    off = HALO - (CONV_KERNEL - 1)
    for jj in range(CONV_KERNEL):
        a, ph = divmod(off + jj, SUBLANES)
        rows = slice(a * SUBLANES, a * SUBLANES + tm)
        tap = hbuf[rows, :] if ph == 0 else hsh[ph - 1, rows, :]
        c = c + tap * dw_ref[jj:jj + 1, :]
    mu = jnp.mean(c, axis=-1, keepdims=True)
    cc = c - mu
    var = jnp.mean(cc * cc, axis=-1, keepdims=True)
    hn = cc * lax.rsqrt(var + EPS) * lng_ref[...] + lnb_ref[...]
    sw = (hn * jax.nn.sigmoid(hn)).astype(MXU_DTYPE)
    yc = jnp.dot(sw, pw_ref[...], preferred_element_type=F32) + pwb_ref[...]

    y = jnp.dot(ya_ref[...], wo_ref[0:ATT_WIDTH, :], preferred_element_type=F32)
    y = y + jnp.dot(yb.astype(MXU_DTYPE), wo_ref[ATT_WIDTH:ATT_WIDTH + POOL_WIDTH, :], preferred_element_type=F32)
    y = y + jnp.dot(yc.astype(MXU_DTYPE), wo_ref[ATT_WIDTH + POOL_WIDTH:, :], preferred_element_type=F32)
    o_ref[...] = x_ref[...] + y


def _mix(x2d, ya, o32, wp, ps, dw, dwb, lng, lnb, pw, pwb, wo, batch, seq, tm):
    t = batch * seq
    nt = seq // tm
    hb = tm // HALO
    full = lambda b, j: (0, 0)
    return pl.pallas_call(
        functools.partial(_mix_kernel, tm=tm),
        grid=(batch, nt),
        in_specs=[
            pl.BlockSpec((tm, D_MODEL), lambda b, j: (b * nt + j, 0)),
            pl.BlockSpec((tm, ATT_WIDTH), lambda b, j: (b * nt + j, 0)),
            pl.BlockSpec((tm, N32), lambda b, j: (b * nt + j, 0)),
            pl.BlockSpec((HALO, N32), lambda b, j: (jnp.maximum((b * nt + j) * hb - 1, 0), 0)),
            pl.BlockSpec((POOL_WIDTH, POOL_WIDTH), full),
            pl.BlockSpec((1, POOL_WIDTH), full),
            pl.BlockSpec((HALO, CONV_WIDTH), full),
            pl.BlockSpec((1, CONV_WIDTH), full),
            pl.BlockSpec((1, CONV_WIDTH), full),
            pl.BlockSpec((1, CONV_WIDTH), full),
            pl.BlockSpec((CONV_WIDTH, CONV_WIDTH), full),
            pl.BlockSpec((1, CONV_WIDTH), full),
            pl.BlockSpec((D_MODEL, D_MODEL), full),
        ],
        out_specs=pl.BlockSpec((tm, D_MODEL), lambda b, j: (b * nt + j, 0)),
        out_shape=jax.ShapeDtypeStruct((t, D_MODEL), F32),
        scratch_shapes=[
            pltpu.VMEM((HALO + tm, POOL_WIDTH), F32),
            pltpu.VMEM((HALO + tm, CONV_WIDTH), F32),
            pltpu.VMEM((SUBLANES - 1, HALO + tm - SUBLANES, CONV_WIDTH), F32),
        ],
        compiler_params=pltpu.CompilerParams(
            dimension_semantics=("parallel", "parallel"), vmem_limit_bytes=VMEM_LIMIT),
        name="mix",
    )(x2d, ya, o32, o32, wp, ps, dw, dwb, lng, lnb, pw, pwb, wo)


def _route(glog, elog):
    lane = lax.broadcasted_iota(jnp.int32, glog.shape, 1)
    lane_f = lane.astype(F32)
    big = float(LANES)
    gl = jnp.where(lane < N_GROUPS, glog, NEG_INF)
    ge = jnp.exp(gl - jnp.max(gl, axis=-1, keepdims=True))
    gp = ge / jnp.sum(ge, axis=-1, keepdims=True)
    p_g = jnp.max(gp, axis=-1, keepdims=True)
    g_sel = jnp.min(jnp.where(gp == p_g, lane_f, big), axis=-1, keepdims=True)
    in_grp = (lane // EXPERTS_PER_GROUP).astype(F32) == g_sel
    el = jnp.where(in_grp, elog, NEG_INF)
    ee = jnp.exp(el - jnp.max(el, axis=-1, keepdims=True))
    ep = ee / jnp.sum(ee, axis=-1, keepdims=True)
    ep = jnp.where(in_grp, ep, -1.0)
    v1 = jnp.max(ep, axis=-1, keepdims=True)
    i1 = jnp.min(jnp.where(ep == v1, lane_f, big), axis=-1, keepdims=True)
    ep2 = jnp.where(lane_f == i1, -1.0, ep)
    v2 = jnp.max(ep2, axis=-1, keepdims=True)
    i2 = jnp.min(jnp.where(ep2 == v2, lane_f, big), axis=-1, keepdims=True)
    den = v1 + v2
    w_e = jnp.where(lane_f == i1, v1 / den, jnp.where(lane_f == i2, v2 / den, 0.0))
    return p_g * w_e, g_sel


def _moe_kernel(x_ref, g2_ref, rgw_ref, rgb_ref, rew_ref, reb_ref, tri_ref, wg_ref, wu_ref, wd_ref, fg_ref,
                o_ref, h_ref, gate3_ref, info_ref, infot_ref, xg_ref, gg_ref, yg_ref, cnt_ref,
                *, tm, final_norm):
    e = pl.program_id(1)
    grp = e // EXPERTS_PER_GROUP
    le = e % EXPERTS_PER_GROUP
    grp_f = grp.astype(F32)

    @pl.when(e == 0)
    def _():
        x = x_ref[...]
        hb = _rms(x, g2_ref[...]).astype(MXU_DTYPE)
        h_ref[...] = hb
        glog = jnp.dot(hb, rgw_ref[...], preferred_element_type=F32) + rgb_ref[...]
        elog = jnp.dot(hb, rew_ref[...], preferred_element_type=F32) + reb_ref[...]
        gate, g_sel = _route(glog, elog)
        g1 = gate.astype(MXU_DTYPE)
        r1 = gate - g1.astype(F32)
        g2 = r1.astype(MXU_DTYPE)
        gate3_ref[0] = g1
        gate3_ref[1] = g2
        gate3_ref[2] = (r1 - g2.astype(F32)).astype(MXU_DTYPE)
        lane = lax.broadcasted_iota(jnp.int32, (tm, LANES), 1)
        member = jnp.where(lane.astype(F32) == g_sel, 1.0, 0.0).astype(MXU_DTYPE)
        ranks = jnp.dot(tri_ref[...], member, preferred_element_type=F32)
        info = jnp.where(lane < N_GROUPS, ranks, jnp.where(lane == N_GROUPS, g_sel, 0.0))
        info_ref[...] = info
        infot_ref[...] = info.T[:8, :]
        for gi in range(N_GROUPS):
            cnt_ref[gi] = ranks[tm - 1, gi].astype(jnp.int32)
        o_ref[...] = x

    n_rows = cnt_ref[grp]
    n_blk = (n_rows + (RB - 1)) // RB

    @pl.when(le == 0)
    def _():
        rank_t = infot_ref[pl.ds(grp, 1), :]
        member_t = infot_ref[N_GROUPS:N_GROUPS + 1, :] == grp_f

        def gather(rb, carry):
            r0 = pl.multiple_of(rb * RB, RB)
            want = (r0 + 1 + lax.broadcasted_iota(jnp.int32, (RB, tm), 0)).astype(F32)
            pick = jnp.where(member_t & (rank_t == want), 1.0, 0.0).astype(MXU_DTYPE)
            xg_ref[pl.ds(r0, RB), :] = jnp.dot(pick, h_ref[...], preferred_element_type=F32).astype(xg_ref.dtype)
            gg_ref[pl.ds(r0, RB), :] = (jnp.dot(pick, gate3_ref[0], preferred_element_type=F32)
                                        + jnp.dot(pick, gate3_ref[1], preferred_element_type=F32)
                                        + jnp.dot(pick, gate3_ref[2], preferred_element_type=F32))
            return carry

        lax.fori_loop(0, n_blk, gather, 0)
        yg_ref[...] = jnp.zeros(yg_ref.shape, F32)

    wg = wg_ref[0]
    wu = wu_ref[0]
    wd = wd_ref[0]

    def expert(rb, carry):
        r0 = pl.multiple_of(rb * RB, RB)
        xb = xg_ref[pl.ds(r0, RB), :]
        a = jnp.dot(xb, wg, preferred_element_type=F32)
        b = jnp.dot(xb, wu, preferred_element_type=F32)
        lane = lax.broadcasted_iota(jnp.int32, (RB, LANES), 1)
        g_col = jnp.sum(jnp.where(lane == e, gg_ref[pl.ds(r0, RB), :], 0.0), axis=-1, keepdims=True)
        act = (a * jax.nn.sigmoid(a)) * b * g_col
        yg_ref[pl.ds(r0, RB), :] += jnp.dot(act.astype(MXU_DTYPE), wd, preferred_element_type=F32)
        return carry

    lax.fori_loop(0, n_blk, expert, 0)

    @pl.when(le == EXPERTS_PER_GROUP - 1)
    def _():
        lane = lax.broadcasted_iota(jnp.int32, (tm, LANES), 1)
        info = info_ref[...]
        rank_c = jnp.sum(jnp.where(lane == grp, info, 0.0), axis=-1, keepdims=True)
        member_c = jnp.sum(jnp.where(lane == N_GROUPS, info, 0.0), axis=-1, keepdims=True) == grp_f

        def scatter(sb, carry):
            r0 = pl.multiple_of(sb * SB, SB)
            want = (r0 + 1 + lax.broadcasted_iota(jnp.int32, (tm, SB), 1)).astype(F32)
            put = jnp.where(member_c & (rank_c == want), 1.0, 0.0).astype(MXU_DTYPE)
            y = yg_ref[pl.ds(r0, SB), :]
            y_hi = y.astype(MXU_DTYPE)
            y_lo = (y - y_hi.astype(F32)).astype(MXU_DTYPE)
            o_ref[...] += (jnp.dot(put, y_hi, preferred_element_type=F32)
                           + jnp.dot(put, y_lo, preferred_element_type=F32))
            return carry

        lax.fori_loop(0, (n_rows + (SB - 1)) // SB, scatter, 0)

    if final_norm:
        @pl.when(e == N_EXPERTS - 1)
        def _():
            o_ref[...] = _rms(o_ref[...], fg_ref[...])


def _moe(x2d, g2, rgw, rgb, rew, reb, tri, wg, wu, wd, fg, layer, final_norm, tm):
    t = x2d.shape[0]
    cap = pl.cdiv(tm, RB) * RB
    full = lambda i, e: (0, 0)
    return pl.pallas_call(
        functools.partial(_moe_kernel, tm=tm, final_norm=final_norm),
        grid=(t // tm, N_EXPERTS),
        in_specs=[
            pl.BlockSpec((tm, D_MODEL), lambda i, e: (i, 0)),
            pl.BlockSpec((1, D_MODEL), full),
            pl.BlockSpec((D_MODEL, LANES), full),
            pl.BlockSpec((1, LANES), full),
            pl.BlockSpec((D_MODEL, LANES), full),
            pl.BlockSpec((1, LANES), full),
            pl.BlockSpec((tm, tm), full),
            pl.BlockSpec((1, D_MODEL, EXPERT_HIDDEN), lambda i, e: (layer * N_EXPERTS + e, 0, 0)),
            pl.BlockSpec((1, D_MODEL, EXPERT_HIDDEN), lambda i, e: (layer * N_EXPERTS + e, 0, 0)),
            pl.BlockSpec((1, EXPERT_HIDDEN, D_MODEL), lambda i, e: (layer * N_EXPERTS + e, 0, 0)),
            pl.BlockSpec((1, D_MODEL), full),
        ],
        out_specs=pl.BlockSpec((tm, D_MODEL), lambda i, e: (i, 0)),
        out_shape=jax.ShapeDtypeStruct((t, D_MODEL), F32),
        scratch_shapes=[
            pltpu.VMEM((tm, D_MODEL), MXU_DTYPE),
            pltpu.VMEM((3, tm, LANES), MXU_DTYPE),
            pltpu.VMEM((tm, LANES), F32),
            pltpu.VMEM((8, tm), F32),
            pltpu.VMEM((cap, D_MODEL), MXU_DTYPE),
            pltpu.VMEM((cap, LANES), F32),
            pltpu.VMEM((cap, D_MODEL), F32),
            pltpu.SMEM((N_GROUPS,), jnp.int32),
        ],
        compiler_params=pltpu.CompilerParams(
            dimension_semantics=("parallel", "arbitrary"), vmem_limit_bytes=MOE_VMEM_LIMIT),
        name="moe",
    )(x2d, g2, rgw, rgb, rew, reb, tri, wg, wu, wd, fg)


def _pad_lanes(w):
    return jnp.pad(w, ((0, 0), (0, LANES - w.shape[-1])))


def _block_diag(blocks):
    g, n, _ = blocks.shape
    out = jnp.zeros((g * n, g * n), blocks.dtype)
    for i in range(g):
        out = out.at[i * n:(i + 1) * n, i * n:(i + 1) * n].set(blocks[i])
    return out


def kernel(x, norm1_g, w_in, pool_w, pool_scale, dw_w, dw_b, conv_ln_g, conv_ln_b, pw_w, pw_b,
           w_out, norm2_g, rg_w, rg_b, re_w, re_b, w_gate, w_up, w_down, final_g):
    batch, seq, d = x.shape
    depth = w_in.shape[0]
    t = batch * seq
    tm = min(512, seq)
    tm_moe = min(1024, t)
    assert d == D_MODEL and seq % QB == 0 and seq % tm == 0 and tm % TK == 0 and t % tm_moe == 0

    o_q, o_k, o_v = 0, ATT_WIDTH, 2 * ATT_WIDTH
    o_qi = 3 * ATT_WIDTH
    o_ki = o_qi + IDX_HEADS * IDX_HEAD_DIM
    o_wi = o_ki + IDX_HEAD_DIM
    o_pool = o_wi + IDX_HEADS
    o_conv = o_pool + POOL_WIDTH

    wg = w_gate.reshape(depth * N_EXPERTS, D_MODEL, EXPERT_HIDDEN).astype(MXU_DTYPE)
    wu = w_up.reshape(depth * N_EXPERTS, D_MODEL, EXPERT_HIDDEN).astype(MXU_DTYPE)
    wd = w_down.reshape(depth * N_EXPERTS, EXPERT_HIDDEN, D_MODEL).astype(MXU_DTYPE)
    fg = final_g.reshape(1, D_MODEL)
    tri = jnp.tri(tm_moe, dtype=MXU_DTYPE)

    xf = x.reshape(t, D_MODEL)
    for l in range(depth):
        w = w_in[l]
        w_ki = w[:, o_ki:o_wi]
        w_cat = jnp.concatenate([w[:, o_k:o_v], w_ki, w_ki, w[:, o_pool:]], axis=1).astype(MXU_DTYPE)
        wt = jnp.concatenate([w[:, o_q:o_k], w[:, o_qi:o_ki], w[:, o_v:o_qi]], axis=1).T.astype(MXU_DTYPE)
        wwit = jnp.pad(w[:, o_wi:o_pool].T, ((0, WI_ROWS - IDX_HEADS), (0, 0))).astype(MXU_DTYPE)
        o16, o32, qt, vt, wit = _proj(xf, norm1_g[l].reshape(1, D_MODEL), w_cat, wt, wwit, tm)

        ya = _attn(qt, o16, vt, wit, batch, seq)

        x1 = _mix(
            xf, ya, o32,
            _block_diag(pool_w[l]).astype(MXU_DTYPE), pool_scale[l].reshape(1, POOL_WIDTH),
            jnp.pad(dw_w[l], ((0, HALO - CONV_KERNEL), (0, 0))), dw_b[l].reshape(1, CONV_WIDTH),
            conv_ln_g[l].reshape(1, CONV_WIDTH), conv_ln_b[l].reshape(1, CONV_WIDTH),
            pw_w[l].astype(MXU_DTYPE), pw_b[l].reshape(1, CONV_WIDTH),
            w_out[l].astype(MXU_DTYPE), batch, seq, tm)

        xf = _moe(
            x1, norm2_g[l].reshape(1, D_MODEL),
            _pad_lanes(rg_w[l]).astype(MXU_DTYPE), _pad_lanes(rg_b[l].reshape(1, N_GROUPS)),
            _pad_lanes(re_w[l]).astype(MXU_DTYPE), _pad_lanes(re_b[l].reshape(1, N_EXPERTS)),
            tri, wg, wu, wd, fg, l, l == depth - 1, tm_moe)
    return xf.reshape(batch, seq, D_MODEL)
```

```python
import functools

import jax
import jax.numpy as jnp
from jax import lax
from jax.experimental import pallas as pl
from jax.experimental.pallas import tpu as pltpu

F32 = jnp.float32
MXU_DTYPE = jnp.bfloat16
COARSE_DTYPE = jnp.bfloat16

D_MODEL = 1024
CHUNK = 64
ATT_HEADS = 8
ATT_HEAD_DIM = 64
ATT_WIDTH = ATT_HEADS * ATT_HEAD_DIM
IDX_HEADS = 8
IDX_HEAD_DIM = 64
TOPK_MAX = 256
POOL_GROUPS = 4
POOL_GROUP_DIM = 64
POOL_WIDTH = POOL_GROUPS * POOL_GROUP_DIM
POOL_WINDOWS = (2, 4, 8, 16)
CONV_WIDTH = 256
CONV_KERNEL = 31
N_GROUPS = 4
EXPERTS_PER_GROUP = 4
N_EXPERTS = N_GROUPS * EXPERTS_PER_GROUP
EXPERT_HIDDEN = 512
EPS = 1e-6

LANES = 128
SUBLANES = 8
INT_MIN = -2 ** 31
NEG_INF = float("-inf")

N16 = ATT_WIDTH + 2 * IDX_HEAD_DIM
KI_COL_BLOCK = ATT_WIDTH // LANES
N32 = POOL_WIDTH + 2 * CONV_WIDTH
NT_ROWS = 3 * ATT_WIDTH
WI_ROWS = 16
VT_ROWS = ATT_HEAD_DIM + 16
KEY_NEG_INF = (0xFF800000 ^ 0x7FFFFFFF) - 2 ** 32

QB = 512
TK = 256
HEAD_GROUP = 8
SEARCH_COLS = 256
CNT16_ROWS = 32
FINE_BITS = 17
CNT_ROWS = 16
HALO = 32
VMEM_LIMIT = 48 * 1024 * 1024
MOE_VMEM_LIMIT = 56 * 1024 * 1024
RB = 288
SB = 256

_NT = (((1,), (1,)), ((), ()))


def _rms(x, g):
    return x * lax.rsqrt(jnp.mean(x * x, axis=-1, keepdims=True) + EPS) * g


def _proj_kernel(x_ref, g_ref, w_ref, wt_ref, wwit_ref, o16_ref, o32_ref, qt_ref, vt_ref, wit_ref, *, tm):
    h = _rms(x_ref[...], g_ref[...]).astype(MXU_DTYPE)
    p = jnp.dot(h, w_ref[...], preferred_element_type=F32)
    o16_ref[...] = p[:, :N16].astype(o16_ref.dtype)
    o32_ref[...] = p[:, N16:]
    pt = lax.dot_general(wt_ref[...], h, _NT, preferred_element_type=F32)
    qt_ref[...] = pt[:2 * ATT_WIDTH, :].astype(qt_ref.dtype)
    ones = jnp.ones((VT_ROWS - ATT_HEAD_DIM, TK), vt_ref.dtype)
    for c in range(tm // TK):
        for hd in range(ATT_HEADS):
            r0 = 2 * ATT_WIDTH + hd * ATT_HEAD_DIM
            vt_ref[c, hd * VT_ROWS:hd * VT_ROWS + ATT_HEAD_DIM, :] = (
                pt[r0:r0 + ATT_HEAD_DIM, c * TK:(c + 1) * TK].astype(vt_ref.dtype))
            vt_ref[c, hd * VT_ROWS + ATT_HEAD_DIM:(hd + 1) * VT_ROWS, :] = ones
    wit_ref[...] = lax.dot_general(wwit_ref[...], h, _NT, preferred_element_type=F32)


def _proj(x2d, g, w, wt, wwit, tm):
    t = x2d.shape[0]
    return pl.pallas_call(
        functools.partial(_proj_kernel, tm=tm),
        grid=(t // tm,),
        in_specs=[
            pl.BlockSpec((tm, D_MODEL), lambda i: (i, 0)),
            pl.BlockSpec((1, D_MODEL), lambda i: (0, 0)),
            pl.BlockSpec((D_MODEL, N16 + N32), lambda i: (0, 0)),
            pl.BlockSpec((NT_ROWS, D_MODEL), lambda i: (0, 0)),
            pl.BlockSpec((WI_ROWS, D_MODEL), lambda i: (0, 0)),
        ],
        out_specs=[
            pl.BlockSpec((tm, N16), lambda i: (i, 0)),
            pl.BlockSpec((tm, N32), lambda i: (i, 0)),
            pl.BlockSpec((2 * ATT_WIDTH, tm), lambda i: (0, i)),
            pl.BlockSpec((tm // TK, ATT_HEADS * VT_ROWS, TK), lambda i: (i, 0, 0)),
            pl.BlockSpec((WI_ROWS, tm), lambda i: (0, i)),
        ],
        out_shape=[
            jax.ShapeDtypeStruct((t, N16), MXU_DTYPE),
            jax.ShapeDtypeStruct((t, N32), F32),
            jax.ShapeDtypeStruct((2 * ATT_WIDTH, t), MXU_DTYPE),
            jax.ShapeDtypeStruct((t // TK, ATT_HEADS * VT_ROWS, TK), MXU_DTYPE),
            jax.ShapeDtypeStruct((WI_ROWS, t), F32),
        ],
        compiler_params=pltpu.CompilerParams(
            dimension_semantics=("parallel",), vmem_limit_bytes=VMEM_LIMIT),
        name="proj",
    )(x2d, g, w, wt, wwit)


def _attn_kernel(qt_ref, qit_ref, k_ref, ki_ref, vt_ref, wit_ref, o_ref,
                 qm_ref, qim_ref, tri_ref, sc_ref, sc16_ref, lg_ref, p_ref, acc_ref, m_ref, l_ref, *, ktop):
    j = pl.program_id(1)
    nkt = (j + 1) * (QB // TK)

    row = lax.broadcasted_iota(jnp.int32, (LANES, QB), 0)
    for h in range(ATT_HEADS):
        pr, half = divmod(h, 2)
        keep = (row < ATT_HEAD_DIM) if half == 0 else (row >= ATT_HEAD_DIM)
        qp = qt_ref[pr * LANES:(pr + 1) * LANES, :].astype(F32) * (ATT_HEAD_DIM ** -0.5)
        qm_ref[h] = jnp.where(keep, qp, 0.0).astype(qm_ref.dtype)
        qip = qit_ref[pr * LANES:(pr + 1) * LANES, :].astype(F32) * (IDX_HEAD_DIM ** -0.5)
        qim_ref[h] = jnp.where(keep, qip, 0.0).astype(qim_ref.dtype)

    r_i = lax.broadcasted_iota(jnp.int32, (TK, TK), 0)
    c_i = lax.broadcasted_iota(jnp.int32, (TK, TK), 1)
    tri_ref[...] = jnp.where(c_i <= r_i, 1.0, 0.0).astype(tri_ref.dtype)

    wt = wit_ref[...]
    q_chunk = (j * QB + lax.broadcasted_iota(jnp.int32, (1, QB), 1)) // CHUNK

    def admissible(k0):
        k_chunk = (k0 + lax.broadcasted_iota(jnp.int32, (TK, 1), 0)) // CHUNK
        return k_chunk <= q_chunk

    def score_tile(kt, carry):
        k0 = pl.multiple_of(kt * TK, TK)
        kit = ki_ref[pl.ds(k0, TK), :]
        acc = jnp.zeros((TK, QB), F32)
        for h in range(IDX_HEADS):
            d = jnp.dot(kit, qim_ref[h], preferred_element_type=F32)
            acc = acc + jnp.maximum(d, 0.0) * wt[h:h + 1, :]
        score = jnp.where(admissible(k0), acc * (IDX_HEADS ** -0.5), NEG_INF)
        half = pl.ds(pl.multiple_of((kt % 2) * TK, TK), TK)
        sc_ref[kt // 2, half, :] = score
        sc16_ref[kt // 2, half, :] = score.astype(sc16_ref.dtype)
        return carry

    lax.fori_loop(0, nkt, score_tile, 0)

    @pl.when(nkt % 2 == 1)
    def _():
        sc_ref[nkt // 2, TK:, :] = jnp.full((TK, QB), NEG_INF, F32)
        sc16_ref[nkt // 2, TK:, :] = jnp.full((TK, QB), NEG_INF, sc16_ref.dtype)

    n_slab = (nkt + 1) // 2

    def key_to_float(key):
        bits = key ^ ((key >> 31) & 0x7FFFFFFF)
        return jnp.where(key < KEY_NEG_INF, NEG_INF, lax.bitcast_convert_type(bits, F32))

    def coarse_key(u):
        key = lax.shift_left(u, 16) ^ INT_MIN
        return jnp.where(key < 0, key | 0xFFFF, key)

    def count(pred_fn, c0):
        def body(kp, cnt):
            ones = jnp.where(pred_fn(sc_ref[kp, :, c0:c0 + SEARCH_COLS]), 1.0, 0.0)
            return cnt + jnp.sum(ones.reshape(2 * TK // CNT_ROWS, CNT_ROWS, SEARCH_COLS), axis=0)
        cnt = lax.fori_loop(0, n_slab, body, jnp.zeros((CNT_ROWS, SEARCH_COLS), F32))
        return jnp.sum(cnt, axis=0, keepdims=True)

    def count16(cand, c0):
        one = jnp.ones((), sc16_ref.dtype)
        zero = jnp.zeros((), sc16_ref.dtype)

        def body(kp, cnt):
            ones = jnp.where(sc16_ref[kp, :, c0:c0 + SEARCH_COLS] >= cand, one, zero)
            parts = [ones[r:r + CNT16_ROWS, :] for r in range(0, 2 * TK, CNT16_ROWS)]
            while len(parts) > 1:
                parts = [a + b for a, b in zip(parts[0::2], parts[1::2])]
            return cnt + parts[0].astype(F32)
        cnt = lax.fori_loop(0, n_slab, body, jnp.zeros((CNT16_ROWS, SEARCH_COLS), F32))
        return jnp.sum(cnt, axis=0, keepdims=True)

    thr_parts, need_parts = [], []
    for c0 in range(0, QB, SEARCH_COLS):
        def coarse_step(i, prefix, c0=c0):
            cand_u = prefix | lax.shift_left(jnp.int32(1), 15 - i)
            cand = key_to_float(coarse_key(cand_u)).astype(sc16_ref.dtype)
            return jnp.where(count16(cand, c0) >= ktop, cand_u, prefix)

        lead = lax.fori_loop(0, 16, coarse_step, jnp.zeros((1, SEARCH_COLS), jnp.int32))
        base = jnp.maximum(coarse_key(lead) - (1 << 15), KEY_NEG_INF)

        def fine_step(i, off, c0=c0, base=base):
            cand_off = off | lax.shift_left(jnp.int32(1), FINE_BITS - 1 - i)
            cand = key_to_float(base + cand_off)
            return jnp.where(count(lambda s: s >= cand, c0) >= ktop, cand_off, off)

        off = lax.fori_loop(0, FINE_BITS, fine_step, jnp.zeros((1, SEARCH_COLS), jnp.int32))
        thr_c = key_to_float(base + off)
        thr_parts.append(thr_c)
        need_parts.append(ktop - count(lambda s: s > thr_c, c0))
    thr = jnp.concatenate(thr_parts, axis=1)
    need = jnp.concatenate(need_parts, axis=1)

    m_ref[...] = jnp.full(m_ref.shape, NEG_INF, F32)
    l_ref[...] = jnp.zeros(l_ref.shape, F32)
    acc_ref[...] = jnp.zeros(acc_ref.shape, F32)

    def attend_tile(kt, eq_before):
        k0 = pl.multiple_of(kt * TK, TK)
        s = sc_ref[kt // 2, pl.ds(pl.multiple_of((kt % 2) * TK, TK), TK), :]
        eq = s == thr
        eqf = jnp.where(eq, 1.0, 0.0)
        incl = jnp.dot(tri_ref[...], eqf.astype(tri_ref.dtype), preferred_element_type=F32)
        sel = ((s > thr) | (eq & ((eq_before + incl) <= need))) & admissible(k0)
        bias = jnp.where(sel, 0.0, NEG_INF)
        reread = jnp.minimum(kt, 0)
        for h0 in range(0, ATT_HEADS, HEAD_GROUP):
            heads = range(h0, h0 + HEAD_GROUP)
            alphas = {}
            for h in heads:
                pr = h // 2
                kp = k_ref[pl.ds(k0, TK), pr * LANES:(pr + 1) * LANES]
                lg = jnp.dot(kp, qm_ref[h], preferred_element_type=F32) + bias
                lg_ref[h] = lg
                m_old = m_ref[h:h + 1, :]
                m_tile = jnp.max(lg.reshape(TK // CNT_ROWS, CNT_ROWS, QB), axis=0)
                m_new = jnp.maximum(m_old, jnp.max(m_tile, axis=0, keepdims=True))
                m_ref[h:h + 1, :] = m_new
                m_safe = jnp.where(m_new == NEG_INF, 0.0, m_new)
                alphas[h] = (jnp.exp(m_old - m_safe), m_safe)
            for h in heads:
                lg = lg_ref[h + reread]
                p_ref[h] = jnp.exp(lg - alphas[h][1]).astype(p_ref.dtype)
            for h in heads:
                alpha = alphas[h][0]
                pv = jnp.dot(vt_ref[kt, h * VT_ROWS:(h + 1) * VT_ROWS, :], p_ref[h],
                             preferred_element_type=F32)
                rows = slice(h * ATT_HEAD_DIM, (h + 1) * ATT_HEAD_DIM)
                acc_ref[rows, :] = alpha * acc_ref[rows, :] + pv[:ATT_HEAD_DIM, :]
                l_ref[h:h + 1, :] = alpha * l_ref[h:h + 1, :] + pv[ATT_HEAD_DIM:ATT_HEAD_DIM + 1, :]
        return eq_before + jnp.sum(eqf, axis=0, keepdims=True)

    lax.fori_loop(0, nkt, attend_tile, jnp.zeros((1, QB), F32))

    for h in range(ATT_HEADS):
        rows = slice(h * ATT_HEAD_DIM, (h + 1) * ATT_HEAD_DIM)
        acc_ref[rows, :] = acc_ref[rows, :] / l_ref[h:h + 1, :]
    o_ref[...] = acc_ref[...].T.astype(o_ref.dtype)


def _attn(qt, o16, vt, wit, batch, seq):
    t = batch * seq
    nq = seq // QB
    nkt = seq // TK
    ktop = min(TOPK_MAX, seq // 4)
    return pl.pallas_call(
        functools.partial(_attn_kernel, ktop=ktop),
        grid=(batch, nq),
        in_specs=[
            pl.BlockSpec((ATT_WIDTH, QB), lambda b, j: (0, b * nq + j)),
            pl.BlockSpec((ATT_WIDTH, QB), lambda b, j: (1, b * nq + j)),
            pl.BlockSpec((seq, ATT_WIDTH), lambda b, j: (b, 0)),
            pl.BlockSpec((seq, LANES), lambda b, j: (b, KI_COL_BLOCK)),
            pl.BlockSpec((nkt, ATT_HEADS * VT_ROWS, TK), lambda b, j: (b, 0, 0)),
            pl.BlockSpec((WI_ROWS, QB), lambda b, j: (0, b * nq + j)),
        ],
        out_specs=pl.BlockSpec((QB, ATT_WIDTH), lambda b, j: (b * nq + j, 0)),
        out_shape=jax.ShapeDtypeStruct((t, ATT_WIDTH), MXU_DTYPE),
        scratch_shapes=[
            pltpu.VMEM((ATT_HEADS, LANES, QB), MXU_DTYPE),
            pltpu.VMEM((IDX_HEADS, LANES, QB), MXU_DTYPE),
            pltpu.VMEM((TK, TK), MXU_DTYPE),
            pltpu.VMEM(((nkt + 1) // 2, 2 * TK, QB), F32),
            pltpu.VMEM(((nkt + 1) // 2, 2 * TK, QB), COARSE_DTYPE),
            pltpu.VMEM((ATT_HEADS, TK, QB), F32),
            pltpu.VMEM((ATT_HEADS, TK, QB), MXU_DTYPE),
            pltpu.VMEM((ATT_WIDTH, QB), F32),
            pltpu.VMEM((ATT_HEADS, QB), F32),
            pltpu.VMEM((ATT_HEADS, QB), F32),
        ],
        compiler_params=pltpu.CompilerParams(
            dimension_semantics=("parallel", "parallel"), vmem_limit_bytes=VMEM_LIMIT),
        name="attn",
    )(qt, qt, o16, o16, vt, wit)


def _mix_kernel(x_ref, ya_ref, cur_ref, halo_ref, wp_ref, ps_ref, dw_ref, dwb_ref, lng_ref, lnb_ref,
                pw_ref, pwb_ref, wo_ref, o_ref, ubuf, hbuf, hsh, *, tm):
    j = pl.program_id(1)
    cur = cur_ref[...]
    halo = jnp.where(j > 0, halo_ref[...], 0.0)

    def glu(z):
        return z[:, POOL_WIDTH:POOL_WIDTH + CONV_WIDTH] * jax.nn.sigmoid(z[:, POOL_WIDTH + CONV_WIDTH:])

    u = cur[:, :POOL_WIDTH]
    ubuf[0:HALO, :] = halo[:, :POOL_WIDTH]
    ubuf[HALO:, :] = u
    hbuf[0:HALO, :] = glu(halo)
    hbuf[HALO:, :] = glu(cur)

    lane = lax.broadcasted_iota(jnp.int32, (tm, LANES), 1)
    upper = lane >= POOL_GROUP_DIM
    s0 = u[:, :LANES]
    s1 = u[:, LANES:]
    for i in range(1, POOL_WINDOWS[3]):
        if i < POOL_WINDOWS[1]:
            sh = ubuf[HALO - i:HALO - i + tm, 0:LANES]
            s0 = s0 + (sh if i < POOL_WINDOWS[0] else jnp.where(upper, sh, 0.0))
        sh = ubuf[HALO - i:HALO - i + tm, LANES:2 * LANES]
        s1 = s1 + (sh if i < POOL_WINDOWS[2] else jnp.where(upper, sh, 0.0))
    t1 = (j * tm + lax.broadcasted_iota(jnp.int32, (tm, LANES), 0) + 1).astype(F32)
    w0 = jnp.where(upper, float(POOL_WINDOWS[1]), float(POOL_WINDOWS[0]))
    w1 = jnp.where(upper, float(POOL_WINDOWS[3]), float(POOL_WINDOWS[2]))
    pooled = jnp.concatenate([s0 / jnp.minimum(t1, w0), s1 / jnp.minimum(t1, w1)], axis=1)
    d = (pooled - u).astype(MXU_DTYPE)
    yb = jnp.dot(d, wp_ref[...], preferred_element_type=F32) * ps_ref[...]

    span = tm + HALO - SUBLANES
    for ph in range(1, SUBLANES):
        hsh[ph - 1] = hbuf[ph:ph + span, :]
    c = jnp.zeros((tm, CONV_WIDTH), F32) + dwb_ref[...]
    off = HALO - (CONV_KERNEL - 1)
    for jj in range(CONV_KERNEL):
        a, ph = divmod(off + jj, SUBLANES)
        rows = slice(a * SUBLANES, a * SUBLANES + tm)
        tap = hbuf[rows, :] if ph == 0 else hsh[ph - 1, rows, :]
        c = c + tap * dw_ref[jj:jj + 1, :]
    mu = jnp.mean(c, axis=-1, keepdims=True)
    cc = c - mu
    var = jnp.mean(cc * cc, axis=-1, keepdims=True)
    hn = cc * lax.rsqrt(var + EPS) * lng_ref[...] + lnb_ref[...]
    sw = (hn * jax.nn.sigmoid(hn)).astype(MXU_DTYPE)
    yc = jnp.dot(sw, pw_ref[...], preferred_element_type=F32) + pwb_ref[...]

    y = jnp.dot(ya_ref[...], wo_ref[0:ATT_WIDTH, :], preferred_element_type=F32)
    y = y + jnp.dot(yb.astype(MXU_DTYPE), wo_ref[ATT_WIDTH:ATT_WIDTH + POOL_WIDTH, :], preferred_element_type=F32)
    y = y + jnp.dot(yc.astype(MXU_DTYPE), wo_ref[ATT_WIDTH + POOL_WIDTH:, :], preferred_element_type=F32)
    o_ref[...] = x_ref[...] + y


def _mix(x2d, ya, o32, wp, ps, dw, dwb, lng, lnb, pw, pwb, wo, batch, seq, tm):
    t = batch * seq
    nt = seq // tm
    hb = tm // HALO
    full = lambda b, j: (0, 0)
    return pl.pallas_call(
        functools.partial(_mix_kernel, tm=tm),
        grid=(batch, nt),
        in_specs=[
            pl.BlockSpec((tm, D_MODEL), lambda b, j: (b * nt + j, 0)),
            pl.BlockSpec((tm, ATT_WIDTH), lambda b, j: (b * nt + j, 0)),
            pl.BlockSpec((tm, N32), lambda b, j: (b * nt + j, 0)),
            pl.BlockSpec((HALO, N32), lambda b, j: (jnp.maximum((b * nt + j) * hb - 1, 0), 0)),
            pl.BlockSpec((POOL_WIDTH, POOL_WIDTH), full),
            pl.BlockSpec((1, POOL_WIDTH), full),
            pl.BlockSpec((HALO, CONV_WIDTH), full),
            pl.BlockSpec((1, CONV_WIDTH), full),
            pl.BlockSpec((1, CONV_WIDTH), full),
            pl.BlockSpec((1, CONV_WIDTH), full),
            pl.BlockSpec((CONV_WIDTH, CONV_WIDTH), full),
            pl.BlockSpec((1, CONV_WIDTH), full),
            pl.BlockSpec((D_MODEL, D_MODEL), full),
        ],
        out_specs=pl.BlockSpec((tm, D_MODEL), lambda b, j: (b * nt + j, 0)),
        out_shape=jax.ShapeDtypeStruct((t, D_MODEL), F32),
        scratch_shapes=[
            pltpu.VMEM((HALO + tm, POOL_WIDTH), F32),
            pltpu.VMEM((HALO + tm, CONV_WIDTH), F32),
            pltpu.VMEM((SUBLANES - 1, HALO + tm - SUBLANES, CONV_WIDTH), F32),
        ],
        compiler_params=pltpu.CompilerParams(
            dimension_semantics=("parallel", "parallel"), vmem_limit_bytes=VMEM_LIMIT),
        name="mix",
    )(x2d, ya, o32, o32, wp, ps, dw, dwb, lng, lnb, pw, pwb, wo)


def _route(glog, elog):
    lane = lax.broadcasted_iota(jnp.int32, glog.shape, 1)
    lane_f = lane.astype(F32)
    big = float(LANES)
    gl = jnp.where(lane < N_GROUPS, glog, NEG_INF)
    ge = jnp.exp(gl - jnp.max(gl, axis=-1, keepdims=True))
    gp = ge / jnp.sum(ge, axis=-1, keepdims=True)
    p_g = jnp.max(gp, axis=-1, keepdims=True)
    g_sel = jnp.min(jnp.where(gp == p_g, lane_f, big), axis=-1, keepdims=True)
    in_grp = (lane // EXPERTS_PER_GROUP).astype(F32) == g_sel
    el = jnp.where(in_grp, elog, NEG_INF)
    ee = jnp.exp(el - jnp.max(el, axis=-1, keepdims=True))
    ep = ee / jnp.sum(ee, axis=-1, keepdims=True)
    ep = jnp.where(in_grp, ep, -1.0)
    v1 = jnp.max(ep, axis=-1, keepdims=True)
    i1 = jnp.min(jnp.where(ep == v1, lane_f, big), axis=-1, keepdims=True)
    ep2 = jnp.where(lane_f == i1, -1.0, ep)
    v2 = jnp.max(ep2, axis=-1, keepdims=True)
    i2 = jnp.min(jnp.where(ep2 == v2, lane_f, big), axis=-1, keepdims=True)
    den = v1 + v2
    w_e = jnp.where(lane_f == i1, v1 / den, jnp.where(lane_f == i2, v2 / den, 0.0))
    return p_g * w_e, g_sel


def _moe_kernel(x_ref, g2_ref, rgw_ref, rgb_ref, rew_ref, reb_ref, tri_ref, wg_ref, wu_ref, wd_ref, fg_ref,
                o_ref, h_ref, gate3_ref, info_ref, infot_ref, xg_ref, gg_ref, yg_ref, cnt_ref,
                *, tm, final_norm):
    e = pl.program_id(1)
    grp = e // EXPERTS_PER_GROUP
    le = e % EXPERTS_PER_GROUP
    grp_f = grp.astype(F32)

    @pl.when(e == 0)
    def _():
        x = x_ref[...]
        hb = _rms(x, g2_ref[...]).astype(MXU_DTYPE)
        h_ref[...] = hb
        glog = jnp.dot(hb, rgw_ref[...], preferred_element_type=F32) + rgb_ref[...]
        elog = jnp.dot(hb, rew_ref[...], preferred_element_type=F32) + reb_ref[...]
        gate, g_sel = _route(glog, elog)
        g1 = gate.astype(MXU_DTYPE)
        r1 = gate - g1.astype(F32)
        g2 = r1.astype(MXU_DTYPE)
        gate3_ref[0] = g1
        gate3_ref[1] = g2
        gate3_ref[2] = (r1 - g2.astype(F32)).astype(MXU_DTYPE)
        lane = lax.broadcasted_iota(jnp.int32, (tm, LANES), 1)
        member = jnp.where(lane.astype(F32) == g_sel, 1.0, 0.0).astype(MXU_DTYPE)
        ranks = jnp.dot(tri_ref[...], member, preferred_element_type=F32)
        info = jnp.where(lane < N_GROUPS, ranks, jnp.where(lane == N_GROUPS, g_sel, 0.0))
        info_ref[...] = info
        infot_ref[...] = info.T[:8, :]
        for gi in range(N_GROUPS):
            cnt_ref[gi] = ranks[tm - 1, gi].astype(jnp.int32)
        o_ref[...] = x

    n_rows = cnt_ref[grp]
    n_blk = (n_rows + (RB - 1)) // RB

    @pl.when(le == 0)
    def _():
        rank_t = infot_ref[pl.ds(grp, 1), :]
        member_t = infot_ref[N_GROUPS:N_GROUPS + 1, :] == grp_f

        def gather(rb, carry):
            r0 = pl.multiple_of(rb * RB, RB)
            want = (r0 + 1 + lax.broadcasted_iota(jnp.int32, (RB, tm), 0)).astype(F32)
            pick = jnp.where(member_t & (rank_t == want), 1.0, 0.0).astype(MXU_DTYPE)
            xg_ref[pl.ds(r0, RB), :] = jnp.dot(pick, h_ref[...], preferred_element_type=F32).astype(xg_ref.dtype)
            gg_ref[pl.ds(r0, RB), :] = (jnp.dot(pick, gate3_ref[0], preferred_element_type=F32)
                                        + jnp.dot(pick, gate3_ref[1], preferred_element_type=F32)
                                        + jnp.dot(pick, gate3_ref[2], preferred_element_type=F32))
            return carry

        lax.fori_loop(0, n_blk, gather, 0)
        yg_ref[...] = jnp.zeros(yg_ref.shape, F32)

    wg = wg_ref[0]
    wu = wu_ref[0]
    wd = wd_ref[0]

    def expert(rb, carry):
        r0 = pl.multiple_of(rb * RB, RB)
        xb = xg_ref[pl.ds(r0, RB), :]
        a = jnp.dot(xb, wg, preferred_element_type=F32)
        b = jnp.dot(xb, wu, preferred_element_type=F32)
        lane = lax.broadcasted_iota(jnp.int32, (RB, LANES), 1)
        g_col = jnp.sum(jnp.where(lane == e, gg_ref[pl.ds(r0, RB), :], 0.0), axis=-1, keepdims=True)
        act = (a * jax.nn.sigmoid(a)) * b * g_col
        yg_ref[pl.ds(r0, RB), :] += jnp.dot(act.astype(MXU_DTYPE), wd, preferred_element_type=F32)
        return carry

    lax.fori_loop(0, n_blk, expert, 0)

    @pl.when(le == EXPERTS_PER_GROUP - 1)
    def _():
        lane = lax.broadcasted_iota(jnp.int32, (tm, LANES), 1)
        info = info_ref[...]
        rank_c = jnp.sum(jnp.where(lane == grp, info, 0.0), axis=-1, keepdims=True)
        member_c = jnp.sum(jnp.where(lane == N_GROUPS, info, 0.0), axis=-1, keepdims=True) == grp_f

        def scatter(sb, carry):
            r0 = pl.multiple_of(sb * SB, SB)
            want = (r0 + 1 + lax.broadcasted_iota(jnp.int32, (tm, SB), 1)).astype(F32)
            put = jnp.where(member_c & (rank_c == want), 1.0, 0.0).astype(MXU_DTYPE)
            y = yg_ref[pl.ds(r0, SB), :]
            y_hi = y.astype(MXU_DTYPE)
            y_lo = (y - y_hi.astype(F32)).astype(MXU_DTYPE)
            o_ref[...] += (jnp.dot(put, y_hi, preferred_element_type=F32)
                           + jnp.dot(put, y_lo, preferred_element_type=F32))
            return carry

        lax.fori_loop(0, (n_rows + (SB - 1)) // SB, scatter, 0)

    if final_norm:
        @pl.when(e == N_EXPERTS - 1)
        def _():
            o_ref[...] = _rms(o_ref[...], fg_ref[...])


def _moe(x2d, g2, rgw, rgb, rew, reb, tri, wg, wu, wd, fg, layer, final_norm, tm):
    t = x2d.shape[0]
    cap = pl.cdiv(tm, RB) * RB
    full = lambda i, e: (0, 0)
    return pl.pallas_call(
        functools.partial(_moe_kernel, tm=tm, final_norm=final_norm),
        grid=(t // tm, N_EXPERTS),
        in_specs=[
            pl.BlockSpec((tm, D_MODEL), lambda i, e: (i, 0)),
            pl.BlockSpec((1, D_MODEL), full),
            pl.BlockSpec((D_MODEL, LANES), full),
            pl.BlockSpec((1, LANES), full),
            pl.BlockSpec((D_MODEL, LANES), full),
            pl.BlockSpec((1, LANES), full),
            pl.BlockSpec((tm, tm), full),
            pl.BlockSpec((1, D_MODEL, EXPERT_HIDDEN), lambda i, e: (layer * N_EXPERTS + e, 0, 0)),
            pl.BlockSpec((1, D_MODEL, EXPERT_HIDDEN), lambda i, e: (layer * N_EXPERTS + e, 0, 0)),
            pl.BlockSpec((1, EXPERT_HIDDEN, D_MODEL), lambda i, e: (layer * N_EXPERTS + e, 0, 0)),
            pl.BlockSpec((1, D_MODEL), full),
        ],
        out_specs=pl.BlockSpec((tm, D_MODEL), lambda i, e: (i, 0)),
        out_shape=jax.ShapeDtypeStruct((t, D_MODEL), F32),
        scratch_shapes=[
            pltpu.VMEM((tm, D_MODEL), MXU_DTYPE),
            pltpu.VMEM((3, tm, LANES), MXU_DTYPE),
            pltpu.VMEM((tm, LANES), F32),
            pltpu.VMEM((8, tm), F32),
            pltpu.VMEM((cap, D_MODEL), MXU_DTYPE),
            pltpu.VMEM((cap, LANES), F32),
            pltpu.VMEM((cap, D_MODEL), F32),
            pltpu.SMEM((N_GROUPS,), jnp.int32),
        ],
        compiler_params=pltpu.CompilerParams(
            dimension_semantics=("parallel", "arbitrary"), vmem_limit_bytes=MOE_VMEM_LIMIT),
        name="moe",
    )(x2d, g2, rgw, rgb, rew, reb, tri, wg, wu, wd, fg)


def _pad_lanes(w):
    return jnp.pad(w, ((0, 0), (0, LANES - w.shape[-1])))


def _block_diag(blocks):
    g, n, _ = blocks.shape
    out = jnp.zeros((g * n, g * n), blocks.dtype)
    for i in range(g):
        out = out.at[i * n:(i + 1) * n, i * n:(i + 1) * n].set(blocks[i])
    return out


def kernel(x, norm1_g, w_in, pool_w, pool_scale, dw_w, dw_b, conv_ln_g, conv_ln_b, pw_w, pw_b,
           w_out, norm2_g, rg_w, rg_b, re_w, re_b, w_gate, w_up, w_down, final_g):
    batch, seq, d = x.shape
    depth = w_in.shape[0]
    t = batch * seq
    tm = min(512, seq)
    tm_moe = min(1024, t)
    assert d == D_MODEL and seq % QB == 0 and seq % tm == 0 and tm % TK == 0 and t % tm_moe == 0

    o_q, o_k, o_v = 0, ATT_WIDTH, 2 * ATT_WIDTH
    o_qi = 3 * ATT_WIDTH
    o_ki = o_qi + IDX_HEADS * IDX_HEAD_DIM
    o_wi = o_ki + IDX_HEAD_DIM
    o_pool = o_wi + IDX_HEADS
    o_conv = o_pool + POOL_WIDTH

    wg = w_gate.reshape(depth * N_EXPERTS, D_MODEL, EXPERT_HIDDEN).astype(MXU_DTYPE)
    wu = w_up.reshape(depth * N_EXPERTS, D_MODEL, EXPERT_HIDDEN).astype(MXU_DTYPE)
    wd = w_down.reshape(depth * N_EXPERTS, EXPERT_HIDDEN, D_MODEL).astype(MXU_DTYPE)
    fg = final_g.reshape(1, D_MODEL)
    tri = jnp.tri(tm_moe, dtype=MXU_DTYPE)

    xf = x.reshape(t, D_MODEL)
    for l in range(depth):
        w = w_in[l]
        w_ki = w[:, o_ki:o_wi]
        w_cat = jnp.concatenate([w[:, o_k:o_v], w_ki, w_ki, w[:, o_pool:]], axis=1).astype(MXU_DTYPE)
        wt = jnp.concatenate([w[:, o_q:o_k], w[:, o_qi:o_ki], w[:, o_v:o_qi]], axis=1).T.astype(MXU_DTYPE)
        wwit = jnp.pad(w[:, o_wi:o_pool].T, ((0, WI_ROWS - IDX_HEADS), (0, 0))).astype(MXU_DTYPE)
        o16, o32, qt, vt, wit = _proj(xf, norm1_g[l].reshape(1, D_MODEL), w_cat, wt, wwit, tm)

        ya = _attn(qt, o16, vt, wit, batch, seq)

        x1 = _mix(
            xf, ya, o32,
            _block_diag(pool_w[l]).astype(MXU_DTYPE), pool_scale[l].reshape(1, POOL_WIDTH),
            jnp.pad(dw_w[l], ((0, HALO - CONV_KERNEL), (0, 0))), dw_b[l].reshape(1, CONV_WIDTH),
            conv_ln_g[l].reshape(1, CONV_WIDTH), conv_ln_b[l].reshape(1, CONV_WIDTH),
            pw_w[l].astype(MXU_DTYPE), pw_b[l].reshape(1, CONV_WIDTH),
            w_out[l].astype(MXU_DTYPE), batch, seq, tm)

        xf = _moe(
            x1, norm2_g[l].reshape(1, D_MODEL),
            _pad_lanes(rg_w[l]).astype(MXU_DTYPE), _pad_lanes(rg_b[l].reshape(1, N_GROUPS)),
            _pad_lanes(re_w[l]).astype(MXU_DTYPE), _pad_lanes(re_b[l].reshape(1, N_EXPERTS)),
            tri, wg, wu, wd, fg, l, l == depth - 1, tm_moe)
    return xf.reshape(batch, seq, D_MODEL)
```

```python
import functools

import jax
import jax.numpy as jnp
from jax import lax
from jax.experimental import pallas as pl
from jax.experimental.pallas import tpu as pltpu

F32 = jnp.float32
MXU_DTYPE = jnp.bfloat16
COARSE_DTYPE = jnp.bfloat16

D_MODEL = 1024
CHUNK = 64
ATT_HEADS = 8
ATT_HEAD_DIM = 64
ATT_WIDTH = ATT_HEADS * ATT_HEAD_DIM
IDX_HEADS = 8
IDX_HEAD_DIM = 64
TOPK_MAX = 256
POOL_GROUPS = 4
POOL_GROUP_DIM = 64
POOL_WIDTH = POOL_GROUPS * POOL_GROUP_DIM
POOL_WINDOWS = (2, 4, 8, 16)
CONV_WIDTH = 256
CONV_KERNEL = 31
N_GROUPS = 4
EXPERTS_PER_GROUP = 4
N_EXPERTS = N_GROUPS * EXPERTS_PER_GROUP
EXPERT_HIDDEN = 512
EPS = 1e-6

LANES = 128
SUBLANES = 8
INT_MIN = -2 ** 31
NEG_INF = float("-inf")

N16 = ATT_WIDTH + 2 * IDX_HEAD_DIM
KI_COL_BLOCK = ATT_WIDTH // LANES
N32 = POOL_WIDTH + 2 * CONV_WIDTH
NT_ROWS = 3 * ATT_WIDTH
WI_ROWS = 16
VT_ROWS = ATT_HEAD_DIM + 16
KEY_NEG_INF = (0xFF800000 ^ 0x7FFFFFFF) - 2 ** 32

QB = 512
TK = 256
HEAD_GROUP = 8
SEARCH_COLS = 256
CNT16_ROWS = 32
FINE_BITS = 17
CNT_ROWS = 16
HALO = 32
VMEM_LIMIT = 48 * 1024 * 1024
MOE_VMEM_LIMIT = 56 * 1024 * 1024
RB = 288
EXPERTS_PER_STEP = 2
SB = 256

_NT = (((1,), (1,)), ((), ()))


def _rms(x, g):
    return x * lax.rsqrt(jnp.mean(x * x, axis=-1, keepdims=True) + EPS) * g


def _proj_kernel(x_ref, g_ref, w_ref, wt_ref, wwit_ref, o16_ref, o32_ref, qt_ref, vt_ref, wit_ref, *, tm):
    h = _rms(x_ref[...], g_ref[...]).astype(MXU_DTYPE)
    p = jnp.dot(h, w_ref[...], preferred_element_type=F32)
    o16_ref[...] = p[:, :N16].astype(o16_ref.dtype)
    o32_ref[...] = p[:, N16:]
    pt = lax.dot_general(wt_ref[...], h, _NT, preferred_element_type=F32)
    qt_ref[...] = pt[:2 * ATT_WIDTH, :].astype(qt_ref.dtype)
    ones = jnp.ones((VT_ROWS - ATT_HEAD_DIM, TK), vt_ref.dtype)
    for c in range(tm // TK):
        for hd in range(ATT_HEADS):
            r0 = 2 * ATT_WIDTH + hd * ATT_HEAD_DIM
            vt_ref[c, hd * VT_ROWS:hd * VT_ROWS + ATT_HEAD_DIM, :] = (
                pt[r0:r0 + ATT_HEAD_DIM, c * TK:(c + 1) * TK].astype(vt_ref.dtype))
            vt_ref[c, hd * VT_ROWS + ATT_HEAD_DIM:(hd + 1) * VT_ROWS, :] = ones
    wit_ref[...] = lax.dot_general(wwit_ref[...], h, _NT, preferred_element_type=F32)


def _proj(x2d, g, w, wt, wwit, tm):
    t = x2d.shape[0]
    return pl.pallas_call(
        functools.partial(_proj_kernel, tm=tm),
        grid=(t // tm,),
        in_specs=[
            pl.BlockSpec((tm, D_MODEL), lambda i: (i, 0)),
            pl.BlockSpec((1, D_MODEL), lambda i: (0, 0)),
            pl.BlockSpec((D_MODEL, N16 + N32), lambda i: (0, 0)),
            pl.BlockSpec((NT_ROWS, D_MODEL), lambda i: (0, 0)),
            pl.BlockSpec((WI_ROWS, D_MODEL), lambda i: (0, 0)),
        ],
        out_specs=[
            pl.BlockSpec((tm, N16), lambda i: (i, 0)),
            pl.BlockSpec((tm, N32), lambda i: (i, 0)),
            pl.BlockSpec((2 * ATT_WIDTH, tm), lambda i: (0, i)),
            pl.BlockSpec((tm // TK, ATT_HEADS * VT_ROWS, TK), lambda i: (i, 0, 0)),
            pl.BlockSpec((WI_ROWS, tm), lambda i: (0, i)),
        ],
        out_shape=[
            jax.ShapeDtypeStruct((t, N16), MXU_DTYPE),
            jax.ShapeDtypeStruct((t, N32), F32),
            jax.ShapeDtypeStruct((2 * ATT_WIDTH, t), MXU_DTYPE),
            jax.ShapeDtypeStruct((t // TK, ATT_HEADS * VT_ROWS, TK), MXU_DTYPE),
            jax.ShapeDtypeStruct((WI_ROWS, t), F32),
        ],
        compiler_params=pltpu.CompilerParams(
            dimension_semantics=("parallel",), vmem_limit_bytes=VMEM_LIMIT),
        name="proj",
    )(x2d, g, w, wt, wwit)


def _attn_kernel(qt_ref, qit_ref, k_ref, ki_ref, vt_ref, wit_ref, o_ref,
                 qm_ref, qim_ref, tri_ref, sc_ref, sc16_ref, lg_ref, p_ref, acc_ref, m_ref, l_ref, *, ktop):
    j = pl.program_id(1)
    nkt = (j + 1) * (QB // TK)

    row = lax.broadcasted_iota(jnp.int32, (LANES, QB), 0)
    for h in range(ATT_HEADS):
        pr, half = divmod(h, 2)
        keep = (row < ATT_HEAD_DIM) if half == 0 else (row >= ATT_HEAD_DIM)
        qp = qt_ref[pr * LANES:(pr + 1) * LANES, :].astype(F32) * (ATT_HEAD_DIM ** -0.5)
        qm_ref[h] = jnp.where(keep, qp, 0.0).astype(qm_ref.dtype)
        qip = qit_ref[pr * LANES:(pr + 1) * LANES, :].astype(F32) * (IDX_HEAD_DIM ** -0.5)
        qim_ref[h] = jnp.where(keep, qip, 0.0).astype(qim_ref.dtype)

    r_i = lax.broadcasted_iota(jnp.int32, (TK, TK), 0)
    c_i = lax.broadcasted_iota(jnp.int32, (TK, TK), 1)
    tri_ref[...] = jnp.where(c_i <= r_i, 1.0, 0.0).astype(tri_ref.dtype)

    wt = wit_ref[...]
    q_chunk = (j * QB + lax.broadcasted_iota(jnp.int32, (1, QB), 1)) // CHUNK

    def admissible(k0):
        k_chunk = (k0 + lax.broadcasted_iota(jnp.int32, (TK, 1), 0)) // CHUNK
        return k_chunk <= q_chunk

    def score_tile(kt, carry):
        k0 = pl.multiple_of(kt * TK, TK)
        kit = ki_ref[pl.ds(k0, TK), :]
        acc = jnp.zeros((TK, QB), F32)
        for h in range(IDX_HEADS):
            d = jnp.dot(kit, qim_ref[h], preferred_element_type=F32)
            acc = acc + jnp.maximum(d, 0.0) * wt[h:h + 1, :]
        score = jnp.where(admissible(k0), acc * (IDX_HEADS ** -0.5), NEG_INF)
        half = pl.ds(pl.multiple_of((kt % 2) * TK, TK), TK)
        sc_ref[kt // 2, half, :] = score
        sc16_ref[kt // 2, half, :] = score.astype(sc16_ref.dtype)
        return carry

    lax.fori_loop(0, nkt, score_tile, 0)

    @pl.when(nkt % 2 == 1)
    def _():
        sc_ref[nkt // 2, TK:, :] = jnp.full((TK, QB), NEG_INF, F32)
        sc16_ref[nkt // 2, TK:, :] = jnp.full((TK, QB), NEG_INF, sc16_ref.dtype)

    n_slab = (nkt + 1) // 2

    def key_to_float(key):
        bits = key ^ ((key >> 31) & 0x7FFFFFFF)
        return jnp.where(key < KEY_NEG_INF, NEG_INF, lax.bitcast_convert_type(bits, F32))

    def coarse_key(u):
        key = lax.shift_left(u, 16) ^ INT_MIN
        return jnp.where(key < 0, key | 0xFFFF, key)

    def count(pred_fn, c0):
        def body(kp, cnt):
            ones = jnp.where(pred_fn(sc_ref[kp, :, c0:c0 + SEARCH_COLS]), 1.0, 0.0)
            return cnt + jnp.sum(ones.reshape(2 * TK // CNT_ROWS, CNT_ROWS, SEARCH_COLS), axis=0)
        cnt = lax.fori_loop(0, n_slab, body, jnp.zeros((CNT_ROWS, SEARCH_COLS), F32))
        return jnp.sum(cnt, axis=0, keepdims=True)

    def count16(cand, c0):
        one = jnp.ones((), sc16_ref.dtype)
        zero = jnp.zeros((), sc16_ref.dtype)

        def body(kp, cnt):
            ones = jnp.where(sc16_ref[kp, :, c0:c0 + SEARCH_COLS] >= cand, one, zero)
            parts = [ones[r:r + CNT16_ROWS, :] for r in range(0, 2 * TK, CNT16_ROWS)]
            while len(parts) > 1:
                parts = [a + b for a, b in zip(parts[0::2], parts[1::2])]
            return cnt + parts[0].astype(F32)
        cnt = lax.fori_loop(0, n_slab, body, jnp.zeros((CNT16_ROWS, SEARCH_COLS), F32))
        return jnp.sum(cnt, axis=0, keepdims=True)

    thr_parts, need_parts = [], []
    for c0 in range(0, QB, SEARCH_COLS):
        def coarse_step(i, prefix, c0=c0):
            cand_u = prefix | lax.shift_left(jnp.int32(1), 15 - i)
            cand = key_to_float(coarse_key(cand_u)).astype(sc16_ref.dtype)
            return jnp.where(count16(cand, c0) >= ktop, cand_u, prefix)

        lead = lax.fori_loop(0, 16, coarse_step, jnp.zeros((1, SEARCH_COLS), jnp.int32))
        base = jnp.maximum(coarse_key(lead) - (1 << 15), KEY_NEG_INF)

        def fine_step(i, off, c0=c0, base=base):
            cand_off = off | lax.shift_left(jnp.int32(1), FINE_BITS - 1 - i)
            cand = key_to_float(base + cand_off)
            return jnp.where(count(lambda s: s >= cand, c0) >= ktop, cand_off, off)

        off = lax.fori_loop(0, FINE_BITS, fine_step, jnp.zeros((1, SEARCH_COLS), jnp.int32))
        thr_c = key_to_float(base + off)
        thr_parts.append(thr_c)
        need_parts.append(ktop - count(lambda s: s > thr_c, c0))
    thr = jnp.concatenate(thr_parts, axis=1)
    need = jnp.concatenate(need_parts, axis=1)

    m_ref[...] = jnp.full(m_ref.shape, NEG_INF, F32)
    l_ref[...] = jnp.zeros(l_ref.shape, F32)
    acc_ref[...] = jnp.zeros(acc_ref.shape, F32)

    def attend_tile(kt, eq_before):
        k0 = pl.multiple_of(kt * TK, TK)
        s = sc_ref[kt // 2, pl.ds(pl.multiple_of((kt % 2) * TK, TK), TK), :]
        eq = s == thr
        eqf = jnp.where(eq, 1.0, 0.0)
        incl = jnp.dot(tri_ref[...], eqf.astype(tri_ref.dtype), preferred_element_type=F32)
        sel = ((s > thr) | (eq & ((eq_before + incl) <= need))) & admissible(k0)
        bias = jnp.where(sel, 0.0, NEG_INF)
        reread = jnp.minimum(kt, 0)
        for h0 in range(0, ATT_HEADS, HEAD_GROUP):
            heads = range(h0, h0 + HEAD_GROUP)
            alphas = {}
            for h in heads:
                pr = h // 2
                kp = k_ref[pl.ds(k0, TK), pr * LANES:(pr + 1) * LANES]
                lg = jnp.dot(kp, qm_ref[h], preferred_element_type=F32) + bias
                lg_ref[h] = lg
                m_old = m_ref[h:h + 1, :]
                m_tile = jnp.max(lg.reshape(TK // CNT_ROWS, CNT_ROWS, QB), axis=0)
                m_new = jnp.maximum(m_old, jnp.max(m_tile, axis=0, keepdims=True))
                m_ref[h:h + 1, :] = m_new
                m_safe = jnp.where(m_new == NEG_INF, 0.0, m_new)
                alphas[h] = (jnp.exp(m_old - m_safe), m_safe)
            for h in heads:
                lg = lg_ref[h + reread]
                p_ref[h] = jnp.exp(lg - alphas[h][1]).astype(p_ref.dtype)
            for h in heads:
                alpha = alphas[h][0]
                pv = jnp.dot(vt_ref[kt, h * VT_ROWS:(h + 1) * VT_ROWS, :], p_ref[h],
                             preferred_element_type=F32)
                rows = slice(h * ATT_HEAD_DIM, (h + 1) * ATT_HEAD_DIM)
                acc_ref[rows, :] = alpha * acc_ref[rows, :] + pv[:ATT_HEAD_DIM, :]
                l_ref[h:h + 1, :] = alpha * l_ref[h:h + 1, :] + pv[ATT_HEAD_DIM:ATT_HEAD_DIM + 1, :]
        return eq_before + jnp.sum(eqf, axis=0, keepdims=True)

    lax.fori_loop(0, nkt, attend_tile, jnp.zeros((1, QB), F32))

    for h in range(ATT_HEADS):
        rows = slice(h * ATT_HEAD_DIM, (h + 1) * ATT_HEAD_DIM)
        acc_ref[rows, :] = acc_ref[rows, :] / l_ref[h:h + 1, :]
    o_ref[...] = acc_ref[...].T.astype(o_ref.dtype)


def _attn(qt, o16, vt, wit, batch, seq):
    t = batch * seq
    nq = seq // QB
    nkt = seq // TK
    ktop = min(TOPK_MAX, seq // 4)
    return pl.pallas_call(
        functools.partial(_attn_kernel, ktop=ktop),
        grid=(batch, nq),
        in_specs=[
            pl.BlockSpec((ATT_WIDTH, QB), lambda b, j: (0, b * nq + j)),
            pl.BlockSpec((ATT_WIDTH, QB), lambda b, j: (1, b * nq + j)),
            pl.BlockSpec((seq, ATT_WIDTH), lambda b, j: (b, 0)),
            pl.BlockSpec((seq, LANES), lambda b, j: (b, KI_COL_BLOCK)),
            pl.BlockSpec((nkt, ATT_HEADS * VT_ROWS, TK), lambda b, j: (b, 0, 0)),
            pl.BlockSpec((WI_ROWS, QB), lambda b, j: (0, b * nq + j)),
        ],
        out_specs=pl.BlockSpec((QB, ATT_WIDTH), lambda b, j: (b * nq + j, 0)),
        out_shape=jax.ShapeDtypeStruct((t, ATT_WIDTH), MXU_DTYPE),
        scratch_shapes=[
            pltpu.VMEM((ATT_HEADS, LANES, QB), MXU_DTYPE),
            pltpu.VMEM((IDX_HEADS, LANES, QB), MXU_DTYPE),
            pltpu.VMEM((TK, TK), MXU_DTYPE),
            pltpu.VMEM(((nkt + 1) // 2, 2 * TK, QB), F32),
            pltpu.VMEM(((nkt + 1) // 2, 2 * TK, QB), COARSE_DTYPE),
            pltpu.VMEM((ATT_HEADS, TK, QB), F32),
            pltpu.VMEM((ATT_HEADS, TK, QB), MXU_DTYPE),
            pltpu.VMEM((ATT_WIDTH, QB), F32),
            pltpu.VMEM((ATT_HEADS, QB), F32),
            pltpu.VMEM((ATT_HEADS, QB), F32),
        ],
        compiler_params=pltpu.CompilerParams(
            dimension_semantics=("parallel", "parallel"), vmem_limit_bytes=VMEM_LIMIT),
        name="attn",
    )(qt, qt, o16, o16, vt, wit)


def _mix_kernel(x_ref, ya_ref, cur_ref, halo_ref, wp_ref, ps_ref, dw_ref, dwb_ref, lng_ref, lnb_ref,
                pw_ref, pwb_ref, wo_ref, o_ref, ubuf, hbuf, hsh, *, tm):
    j = pl.program_id(1)
    cur = cur_ref[...]
    halo = jnp.where(j > 0, halo_ref[...], 0.0)

    def glu(z):
        return z[:, POOL_WIDTH:POOL_WIDTH + CONV_WIDTH] * jax.nn.sigmoid(z[:, POOL_WIDTH + CONV_WIDTH:])

    u = cur[:, :POOL_WIDTH]
    ubuf[0:HALO, :] = halo[:, :POOL_WIDTH]
    ubuf[HALO:, :] = u
    hbuf[0:HALO, :] = glu(halo)
    hbuf[HALO:, :] = glu(cur)

    lane = lax.broadcasted_iota(jnp.int32, (tm, LANES), 1)
    upper = lane >= POOL_GROUP_DIM
    s0 = u[:, :LANES]
    s1 = u[:, LANES:]
    for i in range(1, POOL_WINDOWS[3]):
        if i < POOL_WINDOWS[1]:
            sh = ubuf[HALO - i:HALO - i + tm, 0:LANES]
            s0 = s0 + (sh if i < POOL_WINDOWS[0] else jnp.where(upper, sh, 0.0))
        sh = ubuf[HALO - i:HALO - i + tm, LANES:2 * LANES]
        s1 = s1 + (sh if i < POOL_WINDOWS[2] else jnp.where(upper, sh, 0.0))
    t1 = (j * tm + lax.broadcasted_iota(jnp.int32, (tm, LANES), 0) + 1).astype(F32)
    w0 = jnp.where(upper, float(POOL_WINDOWS[1]), float(POOL_WINDOWS[0]))
    w1 = jnp.where(upper, float(POOL_WINDOWS[3]), float(POOL_WINDOWS[2]))
    pooled = jnp.concatenate([s0 / jnp.minimum(t1, w0), s1 / jnp.minimum(t1, w1)], axis=1)
    d = (pooled - u).astype(MXU_DTYPE)
    yb = jnp.dot(d, wp_ref[...], preferred_element_type=F32) * ps_ref[...]

    span = tm + HALO - SUBLANES
    for ph in range(1, SUBLANES):
        hsh[ph - 1] = hbuf[ph:ph + span, :]
    c = jnp.zeros((tm, CONV_WIDTH), F32) + dwb_ref[...]
    off = HALO - (CONV_KERNEL - 1)
    for jj in range(CONV_KERNEL):
        a, ph = divmod(off + jj, SUBLANES)
        rows = slice(a * SUBLANES, a * SUBLANES + tm)
        tap = hbuf[rows, :] if ph == 0 else hsh[ph - 1, rows, :]
        c = c + tap * dw_ref[jj:jj + 1, :]
    mu = jnp.mean(c, axis=-1, keepdims=True)
    cc = c - mu
    var = jnp.mean(cc * cc, axis=-1, keepdims=True)
    hn = cc * lax.rsqrt(var + EPS) * lng_ref[...] + lnb_ref[...]
    sw = (hn * jax.nn.sigmoid(hn)).astype(MXU_DTYPE)
    yc = jnp.dot(sw, pw_ref[...], preferred_element_type=F32) + pwb_ref[...]

    y = jnp.dot(ya_ref[...], wo_ref[0:ATT_WIDTH, :], preferred_element_type=F32)
    y = y + jnp.dot(yb.astype(MXU_DTYPE), wo_ref[ATT_WIDTH:ATT_WIDTH + POOL_WIDTH, :], preferred_element_type=F32)
    y = y + jnp.dot(yc.astype(MXU_DTYPE), wo_ref[ATT_WIDTH + POOL_WIDTH:, :], preferred_element_type=F32)
    o_ref[...] = x_ref[...] + y


def _mix(x2d, ya, o32, wp, ps, dw, dwb, lng, lnb, pw, pwb, wo, batch, seq, tm):
    t = batch * seq
    nt = seq // tm
    hb = tm // HALO
    full = lambda b, j: (0, 0)
    return pl.pallas_call(
        functools.partial(_mix_kernel, tm=tm),
        grid=(batch, nt),
        in_specs=[
            pl.BlockSpec((tm, D_MODEL), lambda b, j: (b * nt + j, 0)),
            pl.BlockSpec((tm, ATT_WIDTH), lambda b, j: (b * nt + j, 0)),
            pl.BlockSpec((tm, N32), lambda b, j: (b * nt + j, 0)),
            pl.BlockSpec((HALO, N32), lambda b, j: (jnp.maximum((b * nt + j) * hb - 1, 0), 0)),
            pl.BlockSpec((POOL_WIDTH, POOL_WIDTH), full),
            pl.BlockSpec((1, POOL_WIDTH), full),
            pl.BlockSpec((HALO, CONV_WIDTH), full),
            pl.BlockSpec((1, CONV_WIDTH), full),
            pl.BlockSpec((1, CONV_WIDTH), full),
            pl.BlockSpec((1, CONV_WIDTH), full),
            pl.BlockSpec((CONV_WIDTH, CONV_WIDTH), full),
            pl.BlockSpec((1, CONV_WIDTH), full),
            pl.BlockSpec((D_MODEL, D_MODEL), full),
        ],
        out_specs=pl.BlockSpec((tm, D_MODEL), lambda b, j: (b * nt + j, 0)),
        out_shape=jax.ShapeDtypeStruct((t, D_MODEL), F32),
        scratch_shapes=[
            pltpu.VMEM((HALO + tm, POOL_WIDTH), F32),
            pltpu.VMEM((HALO + tm, CONV_WIDTH), F32),
            pltpu.VMEM((SUBLANES - 1, HALO + tm - SUBLANES, CONV_WIDTH), F32),
        ],
        compiler_params=pltpu.CompilerParams(
            dimension_semantics=("parallel", "parallel"), vmem_limit_bytes=VMEM_LIMIT),
        name="mix",
    )(x2d, ya, o32, o32, wp, ps, dw, dwb, lng, lnb, pw, pwb, wo)


def _route(glog, elog):
    lane = lax.broadcasted_iota(jnp.int32, glog.shape, 1)
    lane_f = lane.astype(F32)
    big = float(LANES)
    gl = jnp.where(lane < N_GROUPS, glog, NEG_INF)
    ge = jnp.exp(gl - jnp.max(gl, axis=-1, keepdims=True))
    gp = ge / jnp.sum(ge, axis=-1, keepdims=True)
    p_g = jnp.max(gp, axis=-1, keepdims=True)
    g_sel = jnp.min(jnp.where(gp == p_g, lane_f, big), axis=-1, keepdims=True)
    in_grp = (lane // EXPERTS_PER_GROUP).astype(F32) == g_sel
    el = jnp.where(in_grp, elog, NEG_INF)
    ee = jnp.exp(el - jnp.max(el, axis=-1, keepdims=True))
    ep = ee / jnp.sum(ee, axis=-1, keepdims=True)
    ep = jnp.where(in_grp, ep, -1.0)
    v1 = jnp.max(ep, axis=-1, keepdims=True)
    i1 = jnp.min(jnp.where(ep == v1, lane_f, big), axis=-1, keepdims=True)
    ep2 = jnp.where(lane_f == i1, -1.0, ep)
    v2 = jnp.max(ep2, axis=-1, keepdims=True)
    i2 = jnp.min(jnp.where(ep2 == v2, lane_f, big), axis=-1, keepdims=True)
    den = v1 + v2
    w_e = jnp.where(lane_f == i1, v1 / den, jnp.where(lane_f == i2, v2 / den, 0.0))
    return p_g * w_e, g_sel


def _moe_kernel(x_ref, g2_ref, rgw_ref, rgb_ref, rew_ref, reb_ref, tri_ref, wg_ref, wu_ref, wd_ref, fg_ref,
                o_ref, h_ref, gate3_ref, info_ref, infot_ref, xg_ref, gg_ref, yg_ref, cnt_ref,
                *, tm, final_norm):
    step = pl.program_id(1)
    steps_per_group = EXPERTS_PER_GROUP // EXPERTS_PER_STEP
    grp = step // steps_per_group
    grp_f = grp.astype(F32)

    @pl.when(step == 0)
    def _():
        x = x_ref[...]
        hb = _rms(x, g2_ref[...]).astype(MXU_DTYPE)
        h_ref[...] = hb
        glog = jnp.dot(hb, rgw_ref[...], preferred_element_type=F32) + rgb_ref[...]
        elog = jnp.dot(hb, rew_ref[...], preferred_element_type=F32) + reb_ref[...]
        gate, g_sel = _route(glog, elog)
        g1 = gate.astype(MXU_DTYPE)
        r1 = gate - g1.astype(F32)
        g2 = r1.astype(MXU_DTYPE)
        gate3_ref[0] = g1
        gate3_ref[1] = g2
        gate3_ref[2] = (r1 - g2.astype(F32)).astype(MXU_DTYPE)
        lane = lax.broadcasted_iota(jnp.int32, (tm, LANES), 1)
        member = jnp.where(lane.astype(F32) == g_sel, 1.0, 0.0).astype(MXU_DTYPE)
        ranks = jnp.dot(tri_ref[...], member, preferred_element_type=F32)
        info = jnp.where(lane < N_GROUPS, ranks, jnp.where(lane == N_GROUPS, g_sel, 0.0))
        info_ref[...] = info
        infot_ref[...] = info.T[:8, :]
        for gi in range(N_GROUPS):
            cnt_ref[gi] = ranks[tm - 1, gi].astype(jnp.int32)
        o_ref[...] = x

    n_rows = cnt_ref[grp]
    n_blk = (n_rows + (RB - 1)) // RB

    @pl.when(step % steps_per_group == 0)
    def _():
        rank_t = infot_ref[pl.ds(grp, 1), :]
        member_t = infot_ref[N_GROUPS:N_GROUPS + 1, :] == grp_f

        def gather(rb, carry):
            r0 = pl.multiple_of(rb * RB, RB)
            want = (r0 + 1 + lax.broadcasted_iota(jnp.int32, (RB, tm), 0)).astype(F32)
            pick = jnp.where(member_t & (rank_t == want), 1.0, 0.0).astype(MXU_DTYPE)
            xg_ref[pl.ds(r0, RB), :] = jnp.dot(pick, h_ref[...], preferred_element_type=F32).astype(xg_ref.dtype)
            gg_ref[pl.ds(r0, RB), :] = (jnp.dot(pick, gate3_ref[0], preferred_element_type=F32)
                                        + jnp.dot(pick, gate3_ref[1], preferred_element_type=F32)
                                        + jnp.dot(pick, gate3_ref[2], preferred_element_type=F32))
            return carry

        lax.fori_loop(0, n_blk, gather, 0)
        yg_ref[...] = jnp.zeros(yg_ref.shape, F32)

    def expert(rb, carry):
        r0 = pl.multiple_of(rb * RB, RB)
        xb = xg_ref[pl.ds(r0, RB), :]
        gates = gg_ref[pl.ds(r0, RB), :]
        lane = lax.broadcasted_iota(jnp.int32, (RB, LANES), 1)
        y = jnp.zeros((RB, D_MODEL), F32)
        for ee in range(EXPERTS_PER_STEP):
            a = jnp.dot(xb, wg_ref[ee], preferred_element_type=F32)
            b = jnp.dot(xb, wu_ref[ee], preferred_element_type=F32)
            e = step * EXPERTS_PER_STEP + ee
            g_col = jnp.sum(jnp.where(lane == e, gates, 0.0), axis=-1, keepdims=True)
            act = (a * jax.nn.sigmoid(a)) * b * g_col
            y = y + jnp.dot(act.astype(MXU_DTYPE), wd_ref[ee], preferred_element_type=F32)
        yg_ref[pl.ds(r0, RB), :] += y
        return carry

    lax.fori_loop(0, n_blk, expert, 0)

    @pl.when(step % steps_per_group == steps_per_group - 1)
    def _():
        lane = lax.broadcasted_iota(jnp.int32, (tm, LANES), 1)
        info = info_ref[...]
        rank_c = jnp.sum(jnp.where(lane == grp, info, 0.0), axis=-1, keepdims=True)
        member_c = jnp.sum(jnp.where(lane == N_GROUPS, info, 0.0), axis=-1, keepdims=True) == grp_f

        def scatter(sb, carry):
            r0 = pl.multiple_of(sb * SB, SB)
            want = (r0 + 1 + lax.broadcasted_iota(jnp.int32, (tm, SB), 1)).astype(F32)
            put = jnp.where(member_c & (rank_c == want), 1.0, 0.0).astype(MXU_DTYPE)
            y = yg_ref[pl.ds(r0, SB), :]
            y_hi = y.astype(MXU_DTYPE)
            y_lo = (y - y_hi.astype(F32)).astype(MXU_DTYPE)
            o_ref[...] += (jnp.dot(put, y_hi, preferred_element_type=F32)
                           + jnp.dot(put, y_lo, preferred_element_type=F32))
            return carry

        lax.fori_loop(0, (n_rows + (SB - 1)) // SB, scatter, 0)

    if final_norm:
        @pl.when(step == N_EXPERTS // EXPERTS_PER_STEP - 1)
        def _():
            o_ref[...] = _rms(o_ref[...], fg_ref[...])


def _moe(x2d, g2, rgw, rgb, rew, reb, tri, wg, wu, wd, fg, layer, final_norm, tm):
    t = x2d.shape[0]
    cap = pl.cdiv(tm, RB) * RB
    full = lambda i, e: (0, 0)
    return pl.pallas_call(
        functools.partial(_moe_kernel, tm=tm, final_norm=final_norm),
        grid=(t // tm, N_EXPERTS // EXPERTS_PER_STEP),
        in_specs=[
            pl.BlockSpec((tm, D_MODEL), lambda i, e: (i, 0)),
            pl.BlockSpec((1, D_MODEL), full),
            pl.BlockSpec((D_MODEL, LANES), full),
            pl.BlockSpec((1, LANES), full),
            pl.BlockSpec((D_MODEL, LANES), full),
            pl.BlockSpec((1, LANES), full),
            pl.BlockSpec((tm, tm), full),
            pl.BlockSpec((EXPERTS_PER_STEP, D_MODEL, EXPERT_HIDDEN), lambda i, e: (layer * (N_EXPERTS // EXPERTS_PER_STEP) + e, 0, 0)),
            pl.BlockSpec((EXPERTS_PER_STEP, D_MODEL, EXPERT_HIDDEN), lambda i, e: (layer * (N_EXPERTS // EXPERTS_PER_STEP) + e, 0, 0)),
            pl.BlockSpec((EXPERTS_PER_STEP, EXPERT_HIDDEN, D_MODEL), lambda i, e: (layer * (N_EXPERTS // EXPERTS_PER_STEP) + e, 0, 0)),
            pl.BlockSpec((1, D_MODEL), full),
        ],
        out_specs=pl.BlockSpec((tm, D_MODEL), lambda i, e: (i, 0)),
        out_shape=jax.ShapeDtypeStruct((t, D_MODEL), F32),
        scratch_shapes=[
            pltpu.VMEM((tm, D_MODEL), MXU_DTYPE),
            pltpu.VMEM((3, tm, LANES), MXU_DTYPE),
            pltpu.VMEM((tm, LANES), F32),
            pltpu.VMEM((8, tm), F32),
            pltpu.VMEM((cap, D_MODEL), MXU_DTYPE),
            pltpu.VMEM((cap, LANES), F32),
            pltpu.VMEM((cap, D_MODEL), F32),
            pltpu.SMEM((N_GROUPS,), jnp.int32),
        ],
        compiler_params=pltpu.CompilerParams(
            dimension_semantics=("parallel", "arbitrary"), vmem_limit_bytes=MOE_VMEM_LIMIT),
        name="moe",
    )(x2d, g2, rgw, rgb, rew, reb, tri, wg, wu, wd, fg)


def _pad_lanes(w):
    return jnp.pad(w, ((0, 0), (0, LANES - w.shape[-1])))


def _block_diag(blocks):
    g, n, _ = blocks.shape
    out = jnp.zeros((g * n, g * n), blocks.dtype)
    for i in range(g):
        out = out.at[i * n:(i + 1) * n, i * n:(i + 1) * n].set(blocks[i])
    return out


def kernel(x, norm1_g, w_in, pool_w, pool_scale, dw_w, dw_b, conv_ln_g, conv_ln_b, pw_w, pw_b,
           w_out, norm2_g, rg_w, rg_b, re_w, re_b, w_gate, w_up, w_down, final_g):
    batch, seq, d = x.shape
    depth = w_in.shape[0]
    t = batch * seq
    tm = min(512, seq)
    tm_moe = min(1024, t)
    assert d == D_MODEL and seq % QB == 0 and seq % tm == 0 and tm % TK == 0 and t % tm_moe == 0

    o_q, o_k, o_v = 0, ATT_WIDTH, 2 * ATT_WIDTH
    o_qi = 3 * ATT_WIDTH
    o_ki = o_qi + IDX_HEADS * IDX_HEAD_DIM
    o_wi = o_ki + IDX_HEAD_DIM
    o_pool = o_wi + IDX_HEADS

    wg = w_gate.reshape(depth * N_EXPERTS, D_MODEL, EXPERT_HIDDEN).astype(MXU_DTYPE)
    wu = w_up.reshape(depth * N_EXPERTS, D_MODEL, EXPERT_HIDDEN).astype(MXU_DTYPE)
    wd = w_down.reshape(depth * N_EXPERTS, EXPERT_HIDDEN, D_MODEL).astype(MXU_DTYPE)
    fg = final_g.reshape(1, D_MODEL)
    tri = jnp.tri(tm_moe, dtype=MXU_DTYPE)

    xf = x.reshape(t, D_MODEL)
    for l in range(depth):
        w = w_in[l]
        w_ki = w[:, o_ki:o_wi]
        w_cat = jnp.concatenate([w[:, o_k:o_v], w_ki, w_ki, w[:, o_pool:]], axis=1).astype(MXU_DTYPE)
        wt = jnp.concatenate([w[:, o_q:o_k], w[:, o_qi:o_ki], w[:, o_v:o_qi]], axis=1).T.astype(MXU_DTYPE)
        wwit = jnp.pad(w[:, o_wi:o_pool].T, ((0, WI_ROWS - IDX_HEADS), (0, 0))).astype(MXU_DTYPE)
        o16, o32, qt, vt, wit = _proj(xf, norm1_g[l].reshape(1, D_MODEL), w_cat, wt, wwit, tm)

        ya = _attn(qt, o16, vt, wit, batch, seq)

        x1 = _mix(
            xf, ya, o32,
            _block_diag(pool_w[l]).astype(MXU_DTYPE), pool_scale[l].reshape(1, POOL_WIDTH),
            jnp.pad(dw_w[l], ((0, HALO - CONV_KERNEL), (0, 0))), dw_b[l].reshape(1, CONV_WIDTH),
            conv_ln_g[l].reshape(1, CONV_WIDTH), conv_ln_b[l].reshape(1, CONV_WIDTH),
            pw_w[l].astype(MXU_DTYPE), pw_b[l].reshape(1, CONV_WIDTH),
            w_out[l].astype(MXU_DTYPE), batch, seq, tm)

        xf = _moe(
            x1, norm2_g[l].reshape(1, D_MODEL),
            _pad_lanes(rg_w[l]).astype(MXU_DTYPE), _pad_lanes(rg_b[l].reshape(1, N_GROUPS)),
            _pad_lanes(re_w[l]).astype(MXU_DTYPE), _pad_lanes(re_b[l].reshape(1, N_EXPERTS)),
            tri, wg, wu, wd, fg, l, l == depth - 1, tm_moe)
    return xf.reshape(batch, seq, D_MODEL)
```

```python
import functools

import jax
import jax.numpy as jnp
from jax import lax
from jax.experimental import pallas as pl
from jax.experimental.pallas import tpu as pltpu

F32 = jnp.float32
MXU_DTYPE = jnp.bfloat16
COARSE_DTYPE = jnp.bfloat16

D_MODEL = 1024
CHUNK = 64
ATT_HEADS = 8
ATT_HEAD_DIM = 64
ATT_WIDTH = ATT_HEADS * ATT_HEAD_DIM
IDX_HEADS = 8
IDX_HEAD_DIM = 64
TOPK_MAX = 256
POOL_GROUPS = 4
POOL_GROUP_DIM = 64
POOL_WIDTH = POOL_GROUPS * POOL_GROUP_DIM
POOL_WINDOWS = (2, 4, 8, 16)
CONV_WIDTH = 256
CONV_KERNEL = 31
N_GROUPS = 4
EXPERTS_PER_GROUP = 4
N_EXPERTS = N_GROUPS * EXPERTS_PER_GROUP
EXPERT_HIDDEN = 512
EPS = 1e-6

LANES = 128
SUBLANES = 8
INT_MIN = -2 ** 31
NEG_INF = float("-inf")

N16 = ATT_WIDTH + 2 * IDX_HEAD_DIM
KI_COL_BLOCK = ATT_WIDTH // LANES
N32 = POOL_WIDTH + 2 * CONV_WIDTH
NT_ROWS = 3 * ATT_WIDTH
WI_ROWS = 16
VT_ROWS = ATT_HEAD_DIM + 16
KEY_NEG_INF = (0xFF800000 ^ 0x7FFFFFFF) - 2 ** 32

QB = 512
TK = 256
HEAD_GROUP = 8
SEARCH_COLS = 256
CNT16_ROWS = 32
FINE_BITS = 17
CNT_ROWS = 16
HALO = 32
VMEM_LIMIT = 48 * 1024 * 1024
MOE_VMEM_LIMIT = 58 * 1024 * 1024
RB = 288
EXPERTS_PER_STEP = 4
SB = 256

_NT = (((1,), (1,)), ((), ()))


def _rms(x, g):
    return x * lax.rsqrt(jnp.mean(x * x, axis=-1, keepdims=True) + EPS) * g


def _proj_kernel(x_ref, g_ref, w_ref, wt_ref, wwit_ref, o16_ref, o32_ref, qt_ref, vt_ref, wit_ref, *, tm):
    h = _rms(x_ref[...], g_ref[...]).astype(MXU_DTYPE)
    p = jnp.dot(h, w_ref[...], preferred_element_type=F32)
    o16_ref[...] = p[:, :N16].astype(o16_ref.dtype)
    o32_ref[...] = p[:, N16:]
    pt = lax.dot_general(wt_ref[...], h, _NT, preferred_element_type=F32)
    qt_ref[...] = pt[:2 * ATT_WIDTH, :].astype(qt_ref.dtype)
    ones = jnp.ones((VT_ROWS - ATT_HEAD_DIM, TK), vt_ref.dtype)
    for c in range(tm // TK):
        for hd in range(ATT_HEADS):
            r0 = 2 * ATT_WIDTH + hd * ATT_HEAD_DIM
            vt_ref[c, hd * VT_ROWS:hd * VT_ROWS + ATT_HEAD_DIM, :] = (
                pt[r0:r0 + ATT_HEAD_DIM, c * TK:(c + 1) * TK].astype(vt_ref.dtype))
            vt_ref[c, hd * VT_ROWS + ATT_HEAD_DIM:(hd + 1) * VT_ROWS, :] = ones
    wit_ref[...] = lax.dot_general(wwit_ref[...], h, _NT, preferred_element_type=F32)


def _proj(x2d, g, w, wt, wwit, tm):
    t = x2d.shape[0]
    return pl.pallas_call(
        functools.partial(_proj_kernel, tm=tm),
        grid=(t // tm,),
        in_specs=[
            pl.BlockSpec((tm, D_MODEL), lambda i: (i, 0)),
            pl.BlockSpec((1, D_MODEL), lambda i: (0, 0)),
            pl.BlockSpec((D_MODEL, N16 + N32), lambda i: (0, 0)),
            pl.BlockSpec((NT_ROWS, D_MODEL), lambda i: (0, 0)),
            pl.BlockSpec((WI_ROWS, D_MODEL), lambda i: (0, 0)),
        ],
        out_specs=[
            pl.BlockSpec((tm, N16), lambda i: (i, 0)),
            pl.BlockSpec((tm, N32), lambda i: (i, 0)),
            pl.BlockSpec((2 * ATT_WIDTH, tm), lambda i: (0, i)),
            pl.BlockSpec((tm // TK, ATT_HEADS * VT_ROWS, TK), lambda i: (i, 0, 0)),
            pl.BlockSpec((WI_ROWS, tm), lambda i: (0, i)),
        ],
        out_shape=[
            jax.ShapeDtypeStruct((t, N16), MXU_DTYPE),
            jax.ShapeDtypeStruct((t, N32), F32),
            jax.ShapeDtypeStruct((2 * ATT_WIDTH, t), MXU_DTYPE),
            jax.ShapeDtypeStruct((t // TK, ATT_HEADS * VT_ROWS, TK), MXU_DTYPE),
            jax.ShapeDtypeStruct((WI_ROWS, t), F32),
        ],
        compiler_params=pltpu.CompilerParams(
            dimension_semantics=("parallel",), vmem_limit_bytes=VMEM_LIMIT),
        name="proj",
    )(x2d, g, w, wt, wwit)


def _attn_kernel(qt_ref, qit_ref, k_ref, ki_ref, vt_ref, wit_ref, o_ref,
                 qm_ref, qim_ref, tri_ref, sc_ref, sc16_ref, lg_ref, p_ref, acc_ref, m_ref, l_ref, *, ktop):
    j = pl.program_id(1)
    nkt = (j + 1) * (QB // TK)

    row = lax.broadcasted_iota(jnp.int32, (LANES, QB), 0)
    for h in range(ATT_HEADS):
        pr, half = divmod(h, 2)
        keep = (row < ATT_HEAD_DIM) if half == 0 else (row >= ATT_HEAD_DIM)
        qp = qt_ref[pr * LANES:(pr + 1) * LANES, :].astype(F32) * (ATT_HEAD_DIM ** -0.5)
        qm_ref[h] = jnp.where(keep, qp, 0.0).astype(qm_ref.dtype)
        qip = qit_ref[pr * LANES:(pr + 1) * LANES, :].astype(F32) * (IDX_HEAD_DIM ** -0.5)
        qim_ref[h] = jnp.where(keep, qip, 0.0).astype(qim_ref.dtype)

    r_i = lax.broadcasted_iota(jnp.int32, (TK, TK), 0)
    c_i = lax.broadcasted_iota(jnp.int32, (TK, TK), 1)
    tri_ref[...] = jnp.where(c_i <= r_i, 1.0, 0.0).astype(tri_ref.dtype)

    wt = wit_ref[...]
    q_chunk = (j * QB + lax.broadcasted_iota(jnp.int32, (1, QB), 1)) // CHUNK

    def admissible(k0):
        k_chunk = (k0 + lax.broadcasted_iota(jnp.int32, (TK, 1), 0)) // CHUNK
        return k_chunk <= q_chunk

    def score_tile(kt, carry):
        k0 = pl.multiple_of(kt * TK, TK)
        kit = ki_ref[pl.ds(k0, TK), :]
        acc = jnp.zeros((TK, QB), F32)
        for h in range(IDX_HEADS):
            d = jnp.dot(kit, qim_ref[h], preferred_element_type=F32)
            acc = acc + jnp.maximum(d, 0.0) * wt[h:h + 1, :]
        score = jnp.where(admissible(k0), acc * (IDX_HEADS ** -0.5), NEG_INF)
        half = pl.ds(pl.multiple_of((kt % 2) * TK, TK), TK)
        sc_ref[kt // 2, half, :] = score
        sc16_ref[kt // 2, half, :] = score.astype(sc16_ref.dtype)
        return carry

    lax.fori_loop(0, nkt, score_tile, 0)

    @pl.when(nkt % 2 == 1)
    def _():
        sc_ref[nkt // 2, TK:, :] = jnp.full((TK, QB), NEG_INF, F32)
        sc16_ref[nkt // 2, TK:, :] = jnp.full((TK, QB), NEG_INF, sc16_ref.dtype)

    n_slab = (nkt + 1) // 2

    def key_to_float(key):
        bits = key ^ ((key >> 31) & 0x7FFFFFFF)
        return jnp.where(key < KEY_NEG_INF, NEG_INF, lax.bitcast_convert_type(bits, F32))

    def coarse_key(u):
        key = lax.shift_left(u, 16) ^ INT_MIN
        return jnp.where(key < 0, key | 0xFFFF, key)

    def count(pred_fn, c0):
        def body(kp, cnt):
            ones = jnp.where(pred_fn(sc_ref[kp, :, c0:c0 + SEARCH_COLS]), 1.0, 0.0)
            return cnt + jnp.sum(ones.reshape(2 * TK // CNT_ROWS, CNT_ROWS, SEARCH_COLS), axis=0)
        cnt = lax.fori_loop(0, n_slab, body, jnp.zeros((CNT_ROWS, SEARCH_COLS), F32))
        return jnp.sum(cnt, axis=0, keepdims=True)

    def count16(cand, c0):
        one = jnp.ones((), sc16_ref.dtype)
        zero = jnp.zeros((), sc16_ref.dtype)

        def body(kp, cnt):
            ones = jnp.where(sc16_ref[kp, :, c0:c0 + SEARCH_COLS] >= cand, one, zero)
            parts = [ones[r:r + CNT16_ROWS, :] for r in range(0, 2 * TK, CNT16_ROWS)]
            while len(parts) > 1:
                parts = [a + b for a, b in zip(parts[0::2], parts[1::2])]
            return cnt + parts[0].astype(F32)
        cnt = lax.fori_loop(0, n_slab, body, jnp.zeros((CNT16_ROWS, SEARCH_COLS), F32))
        return jnp.sum(cnt, axis=0, keepdims=True)

    thr_parts, need_parts = [], []
    for c0 in range(0, QB, SEARCH_COLS):
        def coarse_step(i, prefix, c0=c0):
            cand_u = prefix | lax.shift_left(jnp.int32(1), 15 - i)
            cand = key_to_float(coarse_key(cand_u)).astype(sc16_ref.dtype)
            return jnp.where(count16(cand, c0) >= ktop, cand_u, prefix)

        lead = lax.fori_loop(0, 16, coarse_step, jnp.zeros((1, SEARCH_COLS), jnp.int32))
        base = jnp.maximum(coarse_key(lead) - (1 << 15), KEY_NEG_INF)

        def fine_step(i, off, c0=c0, base=base):
            cand_off = off | lax.shift_left(jnp.int32(1), FINE_BITS - 1 - i)
            cand = key_to_float(base + cand_off)
            return jnp.where(count(lambda s: s >= cand, c0) >= ktop, cand_off, off)

        off = lax.fori_loop(0, FINE_BITS, fine_step, jnp.zeros((1, SEARCH_COLS), jnp.int32))
        thr_c = key_to_float(base + off)
        thr_parts.append(thr_c)
        need_parts.append(ktop - count(lambda s: s > thr_c, c0))
    thr = jnp.concatenate(thr_parts, axis=1)
    need = jnp.concatenate(need_parts, axis=1)

    m_ref[...] = jnp.full(m_ref.shape, NEG_INF, F32)
    l_ref[...] = jnp.zeros(l_ref.shape, F32)
    acc_ref[...] = jnp.zeros(acc_ref.shape, F32)

    def attend_tile(kt, eq_before):
        k0 = pl.multiple_of(kt * TK, TK)
        s = sc_ref[kt // 2, pl.ds(pl.multiple_of((kt % 2) * TK, TK), TK), :]
        eq = s == thr
        eqf = jnp.where(eq, 1.0, 0.0)
        incl = jnp.dot(tri_ref[...], eqf.astype(tri_ref.dtype), preferred_element_type=F32)
        sel = ((s > thr) | (eq & ((eq_before + incl) <= need))) & admissible(k0)
        bias = jnp.where(sel, 0.0, NEG_INF)
        reread = jnp.minimum(kt, 0)
        for h0 in range(0, ATT_HEADS, HEAD_GROUP):
            heads = range(h0, h0 + HEAD_GROUP)
            alphas = {}
            for h in heads:
                pr = h // 2
                kp = k_ref[pl.ds(k0, TK), pr * LANES:(pr + 1) * LANES]
                lg = jnp.dot(kp, qm_ref[h], preferred_element_type=F32) + bias
                lg_ref[h] = lg
                m_old = m_ref[h:h + 1, :]
                m_tile = jnp.max(lg.reshape(TK // CNT_ROWS, CNT_ROWS, QB), axis=0)
                m_new = jnp.maximum(m_old, jnp.max(m_tile, axis=0, keepdims=True))
                m_ref[h:h + 1, :] = m_new
                m_safe = jnp.where(m_new == NEG_INF, 0.0, m_new)
                alphas[h] = (jnp.exp(m_old - m_safe), m_safe)
            for h in heads:
                lg = lg_ref[h + reread]
                p_ref[h] = jnp.exp(lg - alphas[h][1]).astype(p_ref.dtype)
            for h in heads:
                alpha = alphas[h][0]
                pv = jnp.dot(vt_ref[kt, h * VT_ROWS:(h + 1) * VT_ROWS, :], p_ref[h],
                             preferred_element_type=F32)
                rows = slice(h * ATT_HEAD_DIM, (h + 1) * ATT_HEAD_DIM)
                acc_ref[rows, :] = alpha * acc_ref[rows, :] + pv[:ATT_HEAD_DIM, :]
                l_ref[h:h + 1, :] = alpha * l_ref[h:h + 1, :] + pv[ATT_HEAD_DIM:ATT_HEAD_DIM + 1, :]
        return eq_before + jnp.sum(eqf, axis=0, keepdims=True)

    lax.fori_loop(0, nkt, attend_tile, jnp.zeros((1, QB), F32))

    for h in range(ATT_HEADS):
        rows = slice(h * ATT_HEAD_DIM, (h + 1) * ATT_HEAD_DIM)
        acc_ref[rows, :] = acc_ref[rows, :] / l_ref[h:h + 1, :]
    o_ref[...] = acc_ref[...].T.astype(o_ref.dtype)


def _attn(qt, o16, vt, wit, batch, seq):
    t = batch * seq
    nq = seq // QB
    nkt = seq // TK
    ktop = min(TOPK_MAX, seq // 4)
    return pl.pallas_call(
        functools.partial(_attn_kernel, ktop=ktop),
        grid=(batch, nq),
        in_specs=[
            pl.BlockSpec((ATT_WIDTH, QB), lambda b, j: (0, b * nq + j)),
            pl.BlockSpec((ATT_WIDTH, QB), lambda b, j: (1, b * nq + j)),
            pl.BlockSpec((seq, ATT_WIDTH), lambda b, j: (b, 0)),
            pl.BlockSpec((seq, LANES), lambda b, j: (b, KI_COL_BLOCK)),
            pl.BlockSpec((nkt, ATT_HEADS * VT_ROWS, TK), lambda b, j: (b, 0, 0)),
            pl.BlockSpec((WI_ROWS, QB), lambda b, j: (0, b * nq + j)),
        ],
        out_specs=pl.BlockSpec((QB, ATT_WIDTH), lambda b, j: (b * nq + j, 0)),
        out_shape=jax.ShapeDtypeStruct((t, ATT_WIDTH), MXU_DTYPE),
        scratch_shapes=[
            pltpu.VMEM((ATT_HEADS, LANES, QB), MXU_DTYPE),
            pltpu.VMEM((IDX_HEADS, LANES, QB), MXU_DTYPE),
            pltpu.VMEM((TK, TK), MXU_DTYPE),
            pltpu.VMEM(((nkt + 1) // 2, 2 * TK, QB), F32),
            pltpu.VMEM(((nkt + 1) // 2, 2 * TK, QB), COARSE_DTYPE),
            pltpu.VMEM((ATT_HEADS, TK, QB), F32),
            pltpu.VMEM((ATT_HEADS, TK, QB), MXU_DTYPE),
            pltpu.VMEM((ATT_WIDTH, QB), F32),
            pltpu.VMEM((ATT_HEADS, QB), F32),
            pltpu.VMEM((ATT_HEADS, QB), F32),
        ],
        compiler_params=pltpu.CompilerParams(
            dimension_semantics=("parallel", "parallel"), vmem_limit_bytes=VMEM_LIMIT),
        name="attn",
    )(qt, qt, o16, o16, vt, wit)


def _mix_kernel(x_ref, ya_ref, cur_ref, halo_ref, wp_ref, ps_ref, dw_ref, dwb_ref, lng_ref, lnb_ref,
                pw_ref, pwb_ref, wo_ref, o_ref, ubuf, hbuf, hsh, *, tm):
    j = pl.program_id(1)
    cur = cur_ref[...]
    halo = jnp.where(j > 0, halo_ref[...], 0.0)

    def glu(z):
        return z[:, POOL_WIDTH:POOL_WIDTH + CONV_WIDTH] * jax.nn.sigmoid(z[:, POOL_WIDTH + CONV_WIDTH:])

    u = cur[:, :POOL_WIDTH]
    ubuf[0:HALO, :] = halo[:, :POOL_WIDTH]
    ubuf[HALO:, :] = u
    hbuf[0:HALO, :] = glu(halo)
    hbuf[HALO:, :] = glu(cur)

    lane = lax.broadcasted_iota(jnp.int32, (tm, LANES), 1)
    upper = lane >= POOL_GROUP_DIM
    s0 = u[:, :LANES]
    s1 = u[:, LANES:]
    for i in range(1, POOL_WINDOWS[3]):
        if i < POOL_WINDOWS[1]:
            sh = ubuf[HALO - i:HALO - i + tm, 0:LANES]
            s0 = s0 + (sh if i < POOL_WINDOWS[0] else jnp.where(upper, sh, 0.0))
        sh = ubuf[HALO - i:HALO - i + tm, LANES:2 * LANES]
        s1 = s1 + (sh if i < POOL_WINDOWS[2] else jnp.where(upper, sh, 0.0))
    t1 = (j * tm + lax.broadcasted_iota(jnp.int32, (tm, LANES), 0) + 1).astype(F32)
    w0 = jnp.where(upper, float(POOL_WINDOWS[1]), float(POOL_WINDOWS[0]))
    w1 = jnp.where(upper, float(POOL_WINDOWS[3]), float(POOL_WINDOWS[2]))
    pooled = jnp.concatenate([s0 / jnp.minimum(t1, w0), s1 / jnp.minimum(t1, w1)], axis=1)
    d = (pooled - u).astype(MXU_DTYPE)
    yb = jnp.dot(d, wp_ref[...], preferred_element_type=F32) * ps_ref[...]

    span = tm + HALO - SUBLANES
    for ph in range(1, SUBLANES):
        hsh[ph - 1] = hbuf[ph:ph + span, :]
    c = jnp.zeros((tm, CONV_WIDTH), F32) + dwb_ref[...]
    off = HALO - (CONV_KERNEL - 1)
    for jj in range(CONV_KERNEL):
        a, ph = divmod(off + jj, SUBLANES)
        rows = slice(a * SUBLANES, a * SUBLANES + tm)
        tap = hbuf[rows, :] if ph == 0 else hsh[ph - 1, rows, :]
        c = c + tap * dw_ref[jj:jj + 1, :]
    mu = jnp.mean(c, axis=-1, keepdims=True)
    cc = c - mu
    var = jnp.mean(cc * cc, axis=-1, keepdims=True)
    hn = cc * lax.rsqrt(var + EPS) * lng_ref[...] + lnb_ref[...]
    sw = (hn * jax.nn.sigmoid(hn)).astype(MXU_DTYPE)
    yc = jnp.dot(sw, pw_ref[...], preferred_element_type=F32) + pwb_ref[...]

    y = jnp.dot(ya_ref[...], wo_ref[0:ATT_WIDTH, :], preferred_element_type=F32)
    y = y + jnp.dot(yb.astype(MXU_DTYPE), wo_ref[ATT_WIDTH:ATT_WIDTH + POOL_WIDTH, :], preferred_element_type=F32)
    y = y + jnp.dot(yc.astype(MXU_DTYPE), wo_ref[ATT_WIDTH + POOL_WIDTH:, :], preferred_element_type=F32)
    o_ref[...] = x_ref[...] + y


def _mix(x2d, ya, o32, wp, ps, dw, dwb, lng, lnb, pw, pwb, wo, batch, seq, tm):
    t = batch * seq
    nt = seq // tm
    hb = tm // HALO
    full = lambda b, j: (0, 0)
    return pl.pallas_call(
        functools.partial(_mix_kernel, tm=tm),
        grid=(batch, nt),
        in_specs=[
            pl.BlockSpec((tm, D_MODEL), lambda b, j: (b * nt + j, 0)),
            pl.BlockSpec((tm, ATT_WIDTH), lambda b, j: (b * nt + j, 0)),
            pl.BlockSpec((tm, N32), lambda b, j: (b * nt + j, 0)),
            pl.BlockSpec((HALO, N32), lambda b, j: (jnp.maximum((b * nt + j) * hb - 1, 0), 0)),
            pl.BlockSpec((POOL_WIDTH, POOL_WIDTH), full),
            pl.BlockSpec((1, POOL_WIDTH), full),
            pl.BlockSpec((HALO, CONV_WIDTH), full),
            pl.BlockSpec((1, CONV_WIDTH), full),
            pl.BlockSpec((1, CONV_WIDTH), full),
            pl.BlockSpec((1, CONV_WIDTH), full),
            pl.BlockSpec((CONV_WIDTH, CONV_WIDTH), full),
            pl.BlockSpec((1, CONV_WIDTH), full),
            pl.BlockSpec((D_MODEL, D_MODEL), full),
        ],
        out_specs=pl.BlockSpec((tm, D_MODEL), lambda b, j: (b * nt + j, 0)),
        out_shape=jax.ShapeDtypeStruct((t, D_MODEL), F32),
        scratch_shapes=[
            pltpu.VMEM((HALO + tm, POOL_WIDTH), F32),
            pltpu.VMEM((HALO + tm, CONV_WIDTH), F32),
            pltpu.VMEM((SUBLANES - 1, HALO + tm - SUBLANES, CONV_WIDTH), F32),
        ],
        compiler_params=pltpu.CompilerParams(
            dimension_semantics=("parallel", "parallel"), vmem_limit_bytes=VMEM_LIMIT),
        name="mix",
    )(x2d, ya, o32, o32, wp, ps, dw, dwb, lng, lnb, pw, pwb, wo)


def _route(glog, elog):
    lane = lax.broadcasted_iota(jnp.int32, glog.shape, 1)
    lane_f = lane.astype(F32)
    big = float(LANES)
    gl = jnp.where(lane < N_GROUPS, glog, NEG_INF)
    ge = jnp.exp(gl - jnp.max(gl, axis=-1, keepdims=True))
    gp = ge / jnp.sum(ge, axis=-1, keepdims=True)
    p_g = jnp.max(gp, axis=-1, keepdims=True)
    g_sel = jnp.min(jnp.where(gp == p_g, lane_f, big), axis=-1, keepdims=True)
    in_grp = (lane // EXPERTS_PER_GROUP).astype(F32) == g_sel
    el = jnp.where(in_grp, elog, NEG_INF)
    ee = jnp.exp(el - jnp.max(el, axis=-1, keepdims=True))
    ep = ee / jnp.sum(ee, axis=-1, keepdims=True)
    ep = jnp.where(in_grp, ep, -1.0)
    v1 = jnp.max(ep, axis=-1, keepdims=True)
    i1 = jnp.min(jnp.where(ep == v1, lane_f, big), axis=-1, keepdims=True)
    ep2 = jnp.where(lane_f == i1, -1.0, ep)
    v2 = jnp.max(ep2, axis=-1, keepdims=True)
    i2 = jnp.min(jnp.where(ep2 == v2, lane_f, big), axis=-1, keepdims=True)
    den = v1 + v2
    w_e = jnp.where(lane_f == i1, v1 / den, jnp.where(lane_f == i2, v2 / den, 0.0))
    return p_g * w_e, g_sel


def _moe_kernel(x_ref, g2_ref, rgw_ref, rgb_ref, rew_ref, reb_ref, tri_ref, wg_ref, wu_ref, wd_ref, fg_ref,
                o_ref, h_ref, gate3_ref, info_ref, infot_ref, xg_ref, gg_ref, yg_ref, cnt_ref,
                *, tm, final_norm):
    step = pl.program_id(1)
    steps_per_group = EXPERTS_PER_GROUP // EXPERTS_PER_STEP
    grp = step // steps_per_group
    grp_f = grp.astype(F32)

    @pl.when(step == 0)
    def _():
        x = x_ref[...]
        hb = _rms(x, g2_ref[...]).astype(MXU_DTYPE)
        h_ref[...] = hb
        glog = jnp.dot(hb, rgw_ref[...], preferred_element_type=F32) + rgb_ref[...]
        elog = jnp.dot(hb, rew_ref[...], preferred_element_type=F32) + reb_ref[...]
        gate, g_sel = _route(glog, elog)
        g1 = gate.astype(MXU_DTYPE)
        r1 = gate - g1.astype(F32)
        g2 = r1.astype(MXU_DTYPE)
        gate3_ref[0] = g1
        gate3_ref[1] = g2
        gate3_ref[2] = (r1 - g2.astype(F32)).astype(MXU_DTYPE)
        lane = lax.broadcasted_iota(jnp.int32, (tm, LANES), 1)
        member = jnp.where(lane.astype(F32) == g_sel, 1.0, 0.0).astype(MXU_DTYPE)
        ranks = jnp.dot(tri_ref[...], member, preferred_element_type=F32)
        info = jnp.where(lane < N_GROUPS, ranks, jnp.where(lane == N_GROUPS, g_sel, 0.0))
        info_ref[...] = info
        infot_ref[...] = info.T[:8, :]
        for gi in range(N_GROUPS):
            cnt_ref[gi] = ranks[tm - 1, gi].astype(jnp.int32)
        o_ref[...] = x

    n_rows = cnt_ref[grp]
    n_blk = (n_rows + (RB - 1)) // RB

    @pl.when(step % steps_per_group == 0)
    def _():
        rank_t = infot_ref[pl.ds(grp, 1), :]
        member_t = infot_ref[N_GROUPS:N_GROUPS + 1, :] == grp_f

        def gather(rb, carry):
            r0 = pl.multiple_of(rb * RB, RB)
            want = (r0 + 1 + lax.broadcasted_iota(jnp.int32, (RB, tm), 0)).astype(F32)
            pick = jnp.where(member_t & (rank_t == want), 1.0, 0.0).astype(MXU_DTYPE)
            xg_ref[pl.ds(r0, RB), :] = jnp.dot(pick, h_ref[...], preferred_element_type=F32).astype(xg_ref.dtype)
            gg_ref[pl.ds(r0, RB), :] = (jnp.dot(pick, gate3_ref[0], preferred_element_type=F32)
                                        + jnp.dot(pick, gate3_ref[1], preferred_element_type=F32)
                                        + jnp.dot(pick, gate3_ref[2], preferred_element_type=F32))
            return carry

        lax.fori_loop(0, n_blk, gather, 0)
        yg_ref[...] = jnp.zeros(yg_ref.shape, F32)

    def expert(rb, carry):
        r0 = pl.multiple_of(rb * RB, RB)
        xb = xg_ref[pl.ds(r0, RB), :]
        gates = gg_ref[pl.ds(r0, RB), :]
        lane = lax.broadcasted_iota(jnp.int32, (RB, LANES), 1)
        y = jnp.zeros((RB, D_MODEL), F32)
        for ee in range(EXPERTS_PER_STEP):
            a = jnp.dot(xb, wg_ref[ee], preferred_element_type=F32)
            b = jnp.dot(xb, wu_ref[ee], preferred_element_type=F32)
            e = step * EXPERTS_PER_STEP + ee
            g_col = jnp.sum(jnp.where(lane == e, gates, 0.0), axis=-1, keepdims=True)
            act = (a * jax.nn.sigmoid(a)) * b * g_col
            y = y + jnp.dot(act.astype(MXU_DTYPE), wd_ref[ee], preferred_element_type=F32)
        yg_ref[pl.ds(r0, RB), :] += y
        return carry

    lax.fori_loop(0, n_blk, expert, 0)

    @pl.when(step % steps_per_group == steps_per_group - 1)
    def _():
        lane = lax.broadcasted_iota(jnp.int32, (tm, LANES), 1)
        info = info_ref[...]
        rank_c = jnp.sum(jnp.where(lane == grp, info, 0.0), axis=-1, keepdims=True)
        member_c = jnp.sum(jnp.where(lane == N_GROUPS, info, 0.0), axis=-1, keepdims=True) == grp_f

        def scatter(sb, carry):
            r0 = pl.multiple_of(sb * SB, SB)
            want = (r0 + 1 + lax.broadcasted_iota(jnp.int32, (tm, SB), 1)).astype(F32)
            put = jnp.where(member_c & (rank_c == want), 1.0, 0.0).astype(MXU_DTYPE)
            y = yg_ref[pl.ds(r0, SB), :]
            y_hi = y.astype(MXU_DTYPE)
            y_lo = (y - y_hi.astype(F32)).astype(MXU_DTYPE)
            o_ref[...] += (jnp.dot(put, y_hi, preferred_element_type=F32)
                           + jnp.dot(put, y_lo, preferred_element_type=F32))
            return carry

        lax.fori_loop(0, (n_rows + (SB - 1)) // SB, scatter, 0)

    if final_norm:
        @pl.when(step == N_EXPERTS // EXPERTS_PER_STEP - 1)
        def _():
            o_ref[...] = _rms(o_ref[...], fg_ref[...])


def _moe(x2d, g2, rgw, rgb, rew, reb, tri, wg, wu, wd, fg, layer, final_norm, tm):
    t = x2d.shape[0]
    cap = pl.cdiv(tm, RB) * RB
    full = lambda i, e: (0, 0)
    return pl.pallas_call(
        functools.partial(_moe_kernel, tm=tm, final_norm=final_norm),
        grid=(t // tm, N_EXPERTS // EXPERTS_PER_STEP),
        in_specs=[
            pl.BlockSpec((tm, D_MODEL), lambda i, e: (i, 0)),
            pl.BlockSpec((1, D_MODEL), full),
            pl.BlockSpec((D_MODEL, LANES), full),
            pl.BlockSpec((1, LANES), full),
            pl.BlockSpec((D_MODEL, LANES), full),
            pl.BlockSpec((1, LANES), full),
            pl.BlockSpec((tm, tm), full, pipeline_mode=pl.Buffered(1)),
            pl.BlockSpec((EXPERTS_PER_STEP, D_MODEL, EXPERT_HIDDEN), lambda i, e: (layer * (N_EXPERTS // EXPERTS_PER_STEP) + e, 0, 0)),
            pl.BlockSpec((EXPERTS_PER_STEP, D_MODEL, EXPERT_HIDDEN), lambda i, e: (layer * (N_EXPERTS // EXPERTS_PER_STEP) + e, 0, 0)),
            pl.BlockSpec((EXPERTS_PER_STEP, EXPERT_HIDDEN, D_MODEL), lambda i, e: (layer * (N_EXPERTS // EXPERTS_PER_STEP) + e, 0, 0)),
            pl.BlockSpec((1, D_MODEL), full),
        ],
        out_specs=pl.BlockSpec((tm, D_MODEL), lambda i, e: (i, 0)),
        out_shape=jax.ShapeDtypeStruct((t, D_MODEL), F32),
        scratch_shapes=[
            pltpu.VMEM((tm, D_MODEL), MXU_DTYPE),
            pltpu.VMEM((3, tm, LANES), MXU_DTYPE),
            pltpu.VMEM((tm, LANES), F32),
            pltpu.VMEM((8, tm), F32),
            pltpu.VMEM((cap, D_MODEL), MXU_DTYPE),
            pltpu.VMEM((cap, LANES), F32),
            pltpu.VMEM((cap, D_MODEL), F32),
            pltpu.SMEM((N_GROUPS,), jnp.int32),
        ],
        compiler_params=pltpu.CompilerParams(
            dimension_semantics=("parallel", "arbitrary"), vmem_limit_bytes=MOE_VMEM_LIMIT),
        name="moe",
    )(x2d, g2, rgw, rgb, rew, reb, tri, wg, wu, wd, fg)


def _pad_lanes(w):
    return jnp.pad(w, ((0, 0), (0, LANES - w.shape[-1])))


def _block_diag(blocks):
    g, n, _ = blocks.shape
    out = jnp.zeros((g * n, g * n), blocks.dtype)
    for i in range(g):
        out = out.at[i * n:(i + 1) * n, i * n:(i + 1) * n].set(blocks[i])
    return out


def kernel(x, norm1_g, w_in, pool_w, pool_scale, dw_w, dw_b, conv_ln_g, conv_ln_b, pw_w, pw_b,
           w_out, norm2_g, rg_w, rg_b, re_w, re_b, w_gate, w_up, w_down, final_g):
    batch, seq, d = x.shape
    depth = w_in.shape[0]
    t = batch * seq
    tm = min(512, seq)
    tm_moe = min(1024, t)
    assert d == D_MODEL and seq % QB == 0 and seq % tm == 0 and tm % TK == 0 and t % tm_moe == 0

    o_q, o_k, o_v = 0, ATT_WIDTH, 2 * ATT_WIDTH
    o_qi = 3 * ATT_WIDTH
    o_ki = o_qi + IDX_HEADS * IDX_HEAD_DIM
    o_wi = o_ki + IDX_HEAD_DIM
    o_pool = o_wi + IDX_HEADS

    wg = w_gate.reshape(depth * N_EXPERTS, D_MODEL, EXPERT_HIDDEN).astype(MXU_DTYPE)
    wu = w_up.reshape(depth * N_EXPERTS, D_MODEL, EXPERT_HIDDEN).astype(MXU_DTYPE)
    wd = w_down.reshape(depth * N_EXPERTS, EXPERT_HIDDEN, D_MODEL).astype(MXU_DTYPE)
    fg = final_g.reshape(1, D_MODEL)
    tri = jnp.tri(tm_moe, dtype=MXU_DTYPE)

    xf = x.reshape(t, D_MODEL)
    for l in range(depth):
        w = w_in[l]
        w_ki = w[:, o_ki:o_wi]
        w_cat = jnp.concatenate([w[:, o_k:o_v], w_ki, w_ki, w[:, o_pool:]], axis=1).astype(MXU_DTYPE)
        wt = jnp.concatenate([w[:, o_q:o_k], w[:, o_qi:o_ki], w[:, o_v:o_qi]], axis=1).T.astype(MXU_DTYPE)
        wwit = jnp.pad(w[:, o_wi:o_pool].T, ((0, WI_ROWS - IDX_HEADS), (0, 0))).astype(MXU_DTYPE)
        o16, o32, qt, vt, wit = _proj(xf, norm1_g[l].reshape(1, D_MODEL), w_cat, wt, wwit, tm)

        ya = _attn(qt, o16, vt, wit, batch, seq)

        x1 = _mix(
            xf, ya, o32,
            _block_diag(pool_w[l]).astype(MXU_DTYPE), pool_scale[l].reshape(1, POOL_WIDTH),
            jnp.pad(dw_w[l], ((0, HALO - CONV_KERNEL), (0, 0))), dw_b[l].reshape(1, CONV_WIDTH),
            conv_ln_g[l].reshape(1, CONV_WIDTH), conv_ln_b[l].reshape(1, CONV_WIDTH),
            pw_w[l].astype(MXU_DTYPE), pw_b[l].reshape(1, CONV_WIDTH),
            w_out[l].astype(MXU_DTYPE), batch, seq, tm)

        xf = _moe(
            x1, norm2_g[l].reshape(1, D_MODEL),
            _pad_lanes(rg_w[l]).astype(MXU_DTYPE), _pad_lanes(rg_b[l].reshape(1, N_GROUPS)),
            _pad_lanes(re_w[l]).astype(MXU_DTYPE), _pad_lanes(re_b[l].reshape(1, N_EXPERTS)),
            tri, wg, wu, wd, fg, l, l == depth - 1, tm_moe)
    return xf.reshape(batch, seq, D_MODEL)
```

```python
import functools

import jax
import jax.numpy as jnp
from jax import lax
from jax.experimental import pallas as pl
from jax.experimental.pallas import tpu as pltpu

F32 = jnp.float32
MXU_DTYPE = jnp.bfloat16
COARSE_DTYPE = jnp.bfloat16

D_MODEL = 1024
CHUNK = 64
ATT_HEADS = 8
ATT_HEAD_DIM = 64
ATT_WIDTH = ATT_HEADS * ATT_HEAD_DIM
IDX_HEADS = 8
IDX_HEAD_DIM = 64
TOPK_MAX = 256
POOL_GROUPS = 4
POOL_GROUP_DIM = 64
POOL_WIDTH = POOL_GROUPS * POOL_GROUP_DIM
POOL_WINDOWS = (2, 4, 8, 16)
CONV_WIDTH = 256
CONV_KERNEL = 31
N_GROUPS = 4
EXPERTS_PER_GROUP = 4
N_EXPERTS = N_GROUPS * EXPERTS_PER_GROUP
EXPERT_HIDDEN = 512
EPS = 1e-6

LANES = 128
SUBLANES = 8
INT_MIN = -2 ** 31
NEG_INF = float("-inf")

N16 = ATT_WIDTH + 2 * IDX_HEAD_DIM
KI_COL_BLOCK = ATT_WIDTH // LANES
N32 = POOL_WIDTH + 2 * CONV_WIDTH
NT_ROWS = 3 * ATT_WIDTH
WI_ROWS = 16
VT_ROWS = ATT_HEAD_DIM + 16
KEY_NEG_INF = (0xFF800000 ^ 0x7FFFFFFF) - 2 ** 32

QB = 512
TK = 256
HEAD_GROUP = 8
SEARCH_COLS = 256
CNT16_ROWS = 32
FINE_BITS = 17
CNT_ROWS = 16
HALO = 32
VMEM_LIMIT = 48 * 1024 * 1024
MOE_VMEM_LIMIT = 58 * 1024 * 1024
RB = 288
EXPERTS_PER_STEP = 4
SB = 256

_NT = (((1,), (1,)), ((), ()))


def _rms(x, g):
    return x * lax.rsqrt(jnp.mean(x * x, axis=-1, keepdims=True) + EPS) * g


def _proj_kernel(x_ref, g_ref, w_ref, wt_ref, wwit_ref, o16_ref, o32_ref, qt_ref, vt_ref, wit_ref, *, tm):
    h = _rms(x_ref[...], g_ref[...]).astype(MXU_DTYPE)
    p = jnp.dot(h, w_ref[...], preferred_element_type=F32)
    o16_ref[...] = p[:, :N16].astype(o16_ref.dtype)
    o32_ref[...] = p[:, N16:]
    pt = lax.dot_general(wt_ref[...], h, _NT, preferred_element_type=F32)
    qt_ref[...] = pt[:2 * ATT_WIDTH, :].astype(qt_ref.dtype)
    ones = jnp.ones((VT_ROWS - ATT_HEAD_DIM, TK), vt_ref.dtype)
    for c in range(tm // TK):
        for hd in range(ATT_HEADS):
            r0 = 2 * ATT_WIDTH + hd * ATT_HEAD_DIM
            vt_ref[c, hd * VT_ROWS:hd * VT_ROWS + ATT_HEAD_DIM, :] = (
                pt[r0:r0 + ATT_HEAD_DIM, c * TK:(c + 1) * TK].astype(vt_ref.dtype))
            vt_ref[c, hd * VT_ROWS + ATT_HEAD_DIM:(hd + 1) * VT_ROWS, :] = ones
    wit_ref[...] = lax.dot_general(wwit_ref[...], h, _NT, preferred_element_type=F32)


def _proj(x2d, g, w, wt, wwit, tm):
    t = x2d.shape[0]
    return pl.pallas_call(
        functools.partial(_proj_kernel, tm=tm),
        grid=(t // tm,),
        in_specs=[
            pl.BlockSpec((tm, D_MODEL), lambda i: (i, 0)),
            pl.BlockSpec((1, D_MODEL), lambda i: (0, 0)),
            pl.BlockSpec((D_MODEL, N16 + N32), lambda i: (0, 0)),
            pl.BlockSpec((NT_ROWS, D_MODEL), lambda i: (0, 0)),
            pl.BlockSpec((WI_ROWS, D_MODEL), lambda i: (0, 0)),
        ],
        out_specs=[
            pl.BlockSpec((tm, N16), lambda i: (i, 0)),
            pl.BlockSpec((tm, N32), lambda i: (i, 0)),
            pl.BlockSpec((2 * ATT_WIDTH, tm), lambda i: (0, i)),
            pl.BlockSpec((tm // TK, ATT_HEADS * VT_ROWS, TK), lambda i: (i, 0, 0)),
            pl.BlockSpec((WI_ROWS, tm), lambda i: (0, i)),
        ],
        out_shape=[
            jax.ShapeDtypeStruct((t, N16), MXU_DTYPE),
            jax.ShapeDtypeStruct((t, N32), F32),
            jax.ShapeDtypeStruct((2 * ATT_WIDTH, t), MXU_DTYPE),
            jax.ShapeDtypeStruct((t // TK, ATT_HEADS * VT_ROWS, TK), MXU_DTYPE),
            jax.ShapeDtypeStruct((WI_ROWS, t), F32),
        ],
        compiler_params=pltpu.CompilerParams(
            dimension_semantics=("parallel",), vmem_limit_bytes=VMEM_LIMIT),
        name="proj",
    )(x2d, g, w, wt, wwit)


def _attn_kernel(qt_ref, qit_ref, k_ref, ki_ref, vt_ref, wit_ref, o_ref,
                 qm_ref, qim_ref, tri_ref, sc_ref, sc16_ref, lg_ref, p_ref, acc_ref, m_ref, l_ref, *, ktop):
    j = pl.program_id(1)
    nkt = (j + 1) * (QB // TK)

    row = lax.broadcasted_iota(jnp.int32, (LANES, QB), 0)
    for h in range(ATT_HEADS):
        pr, half = divmod(h, 2)
        keep = (row < ATT_HEAD_DIM) if half == 0 else (row >= ATT_HEAD_DIM)
        qp = qt_ref[pr * LANES:(pr + 1) * LANES, :].astype(F32) * (ATT_HEAD_DIM ** -0.5)
        qm_ref[h] = jnp.where(keep, qp, 0.0).astype(qm_ref.dtype)
        qip = qit_ref[pr * LANES:(pr + 1) * LANES, :].astype(F32) * (IDX_HEAD_DIM ** -0.5)
        qim_ref[h] = jnp.where(keep, qip, 0.0).astype(qim_ref.dtype)

    r_i = lax.broadcasted_iota(jnp.int32, (TK, TK), 0)
    c_i = lax.broadcasted_iota(jnp.int32, (TK, TK), 1)
    tri_ref[...] = jnp.where(c_i <= r_i, 1.0, 0.0).astype(tri_ref.dtype)

    wt = wit_ref[...]
    q_chunk = (j * QB + lax.broadcasted_iota(jnp.int32, (1, QB), 1)) // CHUNK

    def admissible(k0, rows):
        k_chunk = (k0 + lax.broadcasted_iota(jnp.int32, (rows, 1), 0)) // CHUNK
        return k_chunk <= q_chunk

    def score_slab(kp, carry):
        k0 = pl.multiple_of(kp * (2 * TK), 2 * TK)
        kit = ki_ref[pl.ds(k0, 2 * TK), :]
        acc = jnp.zeros((2 * TK, QB), F32)
        for h in range(IDX_HEADS):
            d = jnp.dot(kit, qim_ref[h], preferred_element_type=F32)
            acc = acc + jnp.maximum(d, 0.0) * wt[h:h + 1, :]
        score = jnp.where(admissible(k0, 2 * TK), acc * (IDX_HEADS ** -0.5), NEG_INF)
        sc_ref[kp] = score
        sc16_ref[kp] = score.astype(sc16_ref.dtype)
        return carry

    lax.fori_loop(0, nkt // 2, score_slab, 0)

    n_slab = nkt // 2

    def key_to_float(key):
        bits = key ^ ((key >> 31) & 0x7FFFFFFF)
        return jnp.where(key < KEY_NEG_INF, NEG_INF, lax.bitcast_convert_type(bits, F32))

    def coarse_key(u):
        key = lax.shift_left(u, 16) ^ INT_MIN
        return jnp.where(key < 0, key | 0xFFFF, key)

    def count(pred_fn, c0):
        def body(kp, cnt):
            ones = jnp.where(pred_fn(sc_ref[kp, :, c0:c0 + SEARCH_COLS]), 1.0, 0.0)
            return cnt + jnp.sum(ones.reshape(2 * TK // CNT_ROWS, CNT_ROWS, SEARCH_COLS), axis=0)
        cnt = lax.fori_loop(0, n_slab, body, jnp.zeros((CNT_ROWS, SEARCH_COLS), F32))
        return jnp.sum(cnt, axis=0, keepdims=True)

    def count16(cand, c0):
        one = jnp.ones((), sc16_ref.dtype)
        zero = jnp.zeros((), sc16_ref.dtype)

        def body(kp, cnt):
            ones = jnp.where(sc16_ref[kp, :, c0:c0 + SEARCH_COLS] >= cand, one, zero)
            parts = [ones[r:r + CNT16_ROWS, :] for r in range(0, 2 * TK, CNT16_ROWS)]
            while len(parts) > 1:
                parts = [a + b for a, b in zip(parts[0::2], parts[1::2])]
            return cnt + parts[0].astype(F32)
        cnt = lax.fori_loop(0, n_slab, body, jnp.zeros((CNT16_ROWS, SEARCH_COLS), F32))
        return jnp.sum(cnt, axis=0, keepdims=True)

    thr_parts, need_parts = [], []
    for c0 in range(0, QB, SEARCH_COLS):
        def coarse_step(i, prefix, c0=c0):
            cand_u = prefix | lax.shift_left(jnp.int32(1), 15 - i)
            cand = key_to_float(coarse_key(cand_u)).astype(sc16_ref.dtype)
            return jnp.where(count16(cand, c0) >= ktop, cand_u, prefix)

        lead = lax.fori_loop(0, 16, coarse_step, jnp.zeros((1, SEARCH_COLS), jnp.int32))
        base = jnp.maximum(coarse_key(lead) - (1 << 15), KEY_NEG_INF)

        def fine_step(i, off, c0=c0, base=base):
            cand_off = off | lax.shift_left(jnp.int32(1), FINE_BITS - 1 - i)
            cand = key_to_float(base + cand_off)
            return jnp.where(count(lambda s: s >= cand, c0) >= ktop, cand_off, off)

        off = lax.fori_loop(0, FINE_BITS, fine_step, jnp.zeros((1, SEARCH_COLS), jnp.int32))
        thr_c = key_to_float(base + off)
        thr_parts.append(thr_c)
        need_parts.append(ktop - count(lambda s: s > thr_c, c0))
    thr = jnp.concatenate(thr_parts, axis=1)
    need = jnp.concatenate(need_parts, axis=1)

    m_ref[...] = jnp.full(m_ref.shape, NEG_INF, F32)
    l_ref[...] = jnp.zeros(l_ref.shape, F32)
    acc_ref[...] = jnp.zeros(acc_ref.shape, F32)

    def attend_tile(kt, eq_before):
        k0 = pl.multiple_of(kt * TK, TK)
        s = sc_ref[kt // 2, pl.ds(pl.multiple_of((kt % 2) * TK, TK), TK), :]
        eq = s == thr
        eqf = jnp.where(eq, 1.0, 0.0)
        incl = jnp.dot(tri_ref[...], eqf.astype(tri_ref.dtype), preferred_element_type=F32)
        sel = ((s > thr) | (eq & ((eq_before + incl) <= need))) & admissible(k0, TK)
        bias = jnp.where(sel, 0.0, NEG_INF)
        reread = jnp.minimum(kt, 0)
        for h0 in range(0, ATT_HEADS, HEAD_GROUP):
            heads = range(h0, h0 + HEAD_GROUP)
            alphas = {}
            for h in heads:
                pr = h // 2
                kp = k_ref[pl.ds(k0, TK), pr * LANES:(pr + 1) * LANES]
                lg = jnp.dot(kp, qm_ref[h], preferred_element_type=F32) + bias
                lg_ref[h] = lg
                m_old = m_ref[h:h + 1, :]
                m_tile = jnp.max(lg.reshape(TK // CNT_ROWS, CNT_ROWS, QB), axis=0)
                m_new = jnp.maximum(m_old, jnp.max(m_tile, axis=0, keepdims=True))
                m_ref[h:h + 1, :] = m_new
                m_safe = jnp.where(m_new == NEG_INF, 0.0, m_new)
                alphas[h] = (jnp.exp(m_old - m_safe), m_safe)
            for h in heads:
                lg = lg_ref[h + reread]
                p_ref[h] = jnp.exp(lg - alphas[h][1]).astype(p_ref.dtype)
            for h in heads:
                alpha = alphas[h][0]
                pv = jnp.dot(vt_ref[kt, h * VT_ROWS:(h + 1) * VT_ROWS, :], p_ref[h],
                             preferred_element_type=F32)
                rows = slice(h * ATT_HEAD_DIM, (h + 1) * ATT_HEAD_DIM)
                acc_ref[rows, :] = alpha * acc_ref[rows, :] + pv[:ATT_HEAD_DIM, :]
                l_ref[h:h + 1, :] = alpha * l_ref[h:h + 1, :] + pv[ATT_HEAD_DIM:ATT_HEAD_DIM + 1, :]
        return eq_before + jnp.sum(eqf, axis=0, keepdims=True)

    lax.fori_loop(0, nkt, attend_tile, jnp.zeros((1, QB), F32))

    for h in range(ATT_HEADS):
        rows = slice(h * ATT_HEAD_DIM, (h + 1) * ATT_HEAD_DIM)
        acc_ref[rows, :] = acc_ref[rows, :] / l_ref[h:h + 1, :]
    o_ref[...] = acc_ref[...].T.astype(o_ref.dtype)


def _attn(qt, o16, vt, wit, batch, seq):
    t = batch * seq
    nq = seq // QB
    nkt = seq // TK
    ktop = min(TOPK_MAX, seq // 4)
    return pl.pallas_call(
        functools.partial(_attn_kernel, ktop=ktop),
        grid=(batch, nq),
        in_specs=[
            pl.BlockSpec((ATT_WIDTH, QB), lambda b, j: (0, b * nq + j)),
            pl.BlockSpec((ATT_WIDTH, QB), lambda b, j: (1, b * nq + j)),
            pl.BlockSpec((seq, ATT_WIDTH), lambda b, j: (b, 0)),
            pl.BlockSpec((seq, LANES), lambda b, j: (b, KI_COL_BLOCK)),
            pl.BlockSpec((nkt, ATT_HEADS * VT_ROWS, TK), lambda b, j: (b, 0, 0)),
            pl.BlockSpec((WI_ROWS, QB), lambda b, j: (0, b * nq + j)),
        ],
        out_specs=pl.BlockSpec((QB, ATT_WIDTH), lambda b, j: (b * nq + j, 0)),
        out_shape=jax.ShapeDtypeStruct((t, ATT_WIDTH), MXU_DTYPE),
        scratch_shapes=[
            pltpu.VMEM((ATT_HEADS, LANES, QB), MXU_DTYPE),
            pltpu.VMEM((IDX_HEADS, LANES, QB), MXU_DTYPE),
            pltpu.VMEM((TK, TK), MXU_DTYPE),
            pltpu.VMEM(((nkt + 1) // 2, 2 * TK, QB), F32),
            pltpu.VMEM(((nkt + 1) // 2, 2 * TK, QB), COARSE_DTYPE),
            pltpu.VMEM((ATT_HEADS, TK, QB), F32),
            pltpu.VMEM((ATT_HEADS, TK, QB), MXU_DTYPE),
            pltpu.VMEM((ATT_WIDTH, QB), F32),
            pltpu.VMEM((ATT_HEADS, QB), F32),
            pltpu.VMEM((ATT_HEADS, QB), F32),
        ],
        compiler_params=pltpu.CompilerParams(
            dimension_semantics=("parallel", "parallel"), vmem_limit_bytes=VMEM_LIMIT),
        name="attn",
    )(qt, qt, o16, o16, vt, wit)


def _mix_kernel(x_ref, ya_ref, cur_ref, halo_ref, wp_ref, ps_ref, dw_ref, dwb_ref, lng_ref, lnb_ref,
                pw_ref, pwb_ref, wo_ref, o_ref, ubuf, hbuf, hsh, *, tm):
    j = pl.program_id(1)
    cur = cur_ref[...]
    halo = jnp.where(j > 0, halo_ref[...], 0.0)

    def glu(z):
        return z[:, POOL_WIDTH:POOL_WIDTH + CONV_WIDTH] * jax.nn.sigmoid(z[:, POOL_WIDTH + CONV_WIDTH:])

    u = cur[:, :POOL_WIDTH]
    ubuf[0:HALO, :] = halo[:, :POOL_WIDTH]
    ubuf[HALO:, :] = u
    hbuf[0:HALO, :] = glu(halo)
    hbuf[HALO:, :] = glu(cur)

    lane = lax.broadcasted_iota(jnp.int32, (tm, LANES), 1)
    upper = lane >= POOL_GROUP_DIM
    s0 = u[:, :LANES]
    s1 = u[:, LANES:]
    for i in range(1, POOL_WINDOWS[3]):
        if i < POOL_WINDOWS[1]:
            sh = ubuf[HALO - i:HALO - i + tm, 0:LANES]
            s0 = s0 + (sh if i < POOL_WINDOWS[0] else jnp.where(upper, sh, 0.0))
        sh = ubuf[HALO - i:HALO - i + tm, LANES:2 * LANES]
        s1 = s1 + (sh if i < POOL_WINDOWS[2] else jnp.where(upper, sh, 0.0))
    t1 = (j * tm + lax.broadcasted_iota(jnp.int32, (tm, LANES), 0) + 1).astype(F32)
    w0 = jnp.where(upper, float(POOL_WINDOWS[1]), float(POOL_WINDOWS[0]))
    w1 = jnp.where(upper, float(POOL_WINDOWS[3]), float(POOL_WINDOWS[2]))
    pooled = jnp.concatenate([s0 / jnp.minimum(t1, w0), s1 / jnp.minimum(t1, w1)], axis=1)
    d = (pooled - u).astype(MXU_DTYPE)
    yb = jnp.dot(d, wp_ref[...], preferred_element_type=F32) * ps_ref[...]

    span = tm + HALO - SUBLANES
    for ph in range(1, SUBLANES):
        hsh[ph - 1] = hbuf[ph:ph + span, :]
    c = jnp.zeros((tm, CONV_WIDTH), F32) + dwb_ref[...]
    off = HALO - (CONV_KERNEL - 1)
    for jj in range(CONV_KERNEL):
        a, ph = divmod(off + jj, SUBLANES)
        rows = slice(a * SUBLANES, a * SUBLANES + tm)
        tap = hbuf[rows, :] if ph == 0 else hsh[ph - 1, rows, :]
        c = c + tap * dw_ref[jj:jj + 1, :]
    mu = jnp.mean(c, axis=-1, keepdims=True)
    cc = c - mu
    var = jnp.mean(cc * cc, axis=-1, keepdims=True)
    hn = cc * lax.rsqrt(var + EPS) * lng_ref[...] + lnb_ref[...]
    sw = (hn * jax.nn.sigmoid(hn)).astype(MXU_DTYPE)
    yc = jnp.dot(sw, pw_ref[...], preferred_element_type=F32) + pwb_ref[...]

    y = jnp.dot(ya_ref[...], wo_ref[0:ATT_WIDTH, :], preferred_element_type=F32)
    y = y + jnp.dot(yb.astype(MXU_DTYPE), wo_ref[ATT_WIDTH:ATT_WIDTH + POOL_WIDTH, :], preferred_element_type=F32)
    y = y + jnp.dot(yc.astype(MXU_DTYPE), wo_ref[ATT_WIDTH + POOL_WIDTH:, :], preferred_element_type=F32)
    o_ref[...] = x_ref[...] + y


def _mix(x2d, ya, o32, wp, ps, dw, dwb, lng, lnb, pw, pwb, wo, batch, seq, tm):
    t = batch * seq
    nt = seq // tm
    hb = tm // HALO
    full = lambda b, j: (0, 0)
    return pl.pallas_call(
        functools.partial(_mix_kernel, tm=tm),
        grid=(batch, nt),
        in_specs=[
            pl.BlockSpec((tm, D_MODEL), lambda b, j: (b * nt + j, 0)),
            pl.BlockSpec((tm, ATT_WIDTH), lambda b, j: (b * nt + j, 0)),
            pl.BlockSpec((tm, N32), lambda b, j: (b * nt + j, 0)),
            pl.BlockSpec((HALO, N32), lambda b, j: (jnp.maximum((b * nt + j) * hb - 1, 0), 0)),
            pl.BlockSpec((POOL_WIDTH, POOL_WIDTH), full),
            pl.BlockSpec((1, POOL_WIDTH), full),
            pl.BlockSpec((HALO, CONV_WIDTH), full),
            pl.BlockSpec((1, CONV_WIDTH), full),
            pl.BlockSpec((1, CONV_WIDTH), full),
            pl.BlockSpec((1, CONV_WIDTH), full),
            pl.BlockSpec((CONV_WIDTH, CONV_WIDTH), full),
            pl.BlockSpec((1, CONV_WIDTH), full),
            pl.BlockSpec((D_MODEL, D_MODEL), full),
        ],
        out_specs=pl.BlockSpec((tm, D_MODEL), lambda b, j: (b * nt + j, 0)),
        out_shape=jax.ShapeDtypeStruct((t, D_MODEL), F32),
        scratch_shapes=[
            pltpu.VMEM((HALO + tm, POOL_WIDTH), F32),
            pltpu.VMEM((HALO + tm, CONV_WIDTH), F32),
            pltpu.VMEM((SUBLANES - 1, HALO + tm - SUBLANES, CONV_WIDTH), F32),
        ],
        compiler_params=pltpu.CompilerParams(
            dimension_semantics=("parallel", "parallel"), vmem_limit_bytes=VMEM_LIMIT),
        name="mix",
    )(x2d, ya, o32, o32, wp, ps, dw, dwb, lng, lnb, pw, pwb, wo)


def _route(glog, elog):
    lane = lax.broadcasted_iota(jnp.int32, glog.shape, 1)
    lane_f = lane.astype(F32)
    big = float(LANES)
    gl = jnp.where(lane < N_GROUPS, glog, NEG_INF)
    ge = jnp.exp(gl - jnp.max(gl, axis=-1, keepdims=True))
    gp = ge / jnp.sum(ge, axis=-1, keepdims=True)
    p_g = jnp.max(gp, axis=-1, keepdims=True)
    g_sel = jnp.min(jnp.where(gp == p_g, lane_f, big), axis=-1, keepdims=True)
    in_grp = (lane // EXPERTS_PER_GROUP).astype(F32) == g_sel
    el = jnp.where(in_grp, elog, NEG_INF)
    ee = jnp.exp(el - jnp.max(el, axis=-1, keepdims=True))
    ep = ee / jnp.sum(ee, axis=-1, keepdims=True)
    ep = jnp.where(in_grp, ep, -1.0)
    v1 = jnp.max(ep, axis=-1, keepdims=True)
    i1 = jnp.min(jnp.where(ep == v1, lane_f, big), axis=-1, keepdims=True)
    ep2 = jnp.where(lane_f == i1, -1.0, ep)
    v2 = jnp.max(ep2, axis=-1, keepdims=True)
    i2 = jnp.min(jnp.where(ep2 == v2, lane_f, big), axis=-1, keepdims=True)
    den = v1 + v2
    w_e = jnp.where(lane_f == i1, v1 / den, jnp.where(lane_f == i2, v2 / den, 0.0))
    return p_g * w_e, g_sel


def _moe_kernel(x_ref, g2_ref, rgw_ref, rgb_ref, rew_ref, reb_ref, tri_ref, wg_ref, wu_ref, wd_ref, fg_ref,
                o_ref, h_ref, gate3_ref, info_ref, infot_ref, xg_ref, gg_ref, yg_ref, cnt_ref,
                *, tm, final_norm):
    step = pl.program_id(1)
    steps_per_group = EXPERTS_PER_GROUP // EXPERTS_PER_STEP
    grp = step // steps_per_group
    grp_f = grp.astype(F32)

    @pl.when(step == 0)
    def _():
        x = x_ref[...]
        hb = _rms(x, g2_ref[...]).astype(MXU_DTYPE)
        h_ref[...] = hb
        glog = jnp.dot(hb, rgw_ref[...], preferred_element_type=F32) + rgb_ref[...]
        elog = jnp.dot(hb, rew_ref[...], preferred_element_type=F32) + reb_ref[...]
        gate, g_sel = _route(glog, elog)
        g1 = gate.astype(MXU_DTYPE)
        r1 = gate - g1.astype(F32)
        g2 = r1.astype(MXU_DTYPE)
        gate3_ref[0] = g1
        gate3_ref[1] = g2
        gate3_ref[2] = (r1 - g2.astype(F32)).astype(MXU_DTYPE)
        lane = lax.broadcasted_iota(jnp.int32, (tm, LANES), 1)
        member = jnp.where(lane.astype(F32) == g_sel, 1.0, 0.0).astype(MXU_DTYPE)
        ranks = jnp.dot(tri_ref[...], member, preferred_element_type=F32)
        info = jnp.where(lane < N_GROUPS, ranks, jnp.where(lane == N_GROUPS, g_sel, 0.0))
        info_ref[...] = info
        infot_ref[...] = info.T[:8, :]
        for gi in range(N_GROUPS):
            cnt_ref[gi] = ranks[tm - 1, gi].astype(jnp.int32)
        o_ref[...] = x

    n_rows = cnt_ref[grp]
    n_blk = (n_rows + (RB - 1)) // RB

    @pl.when(step % steps_per_group == 0)
    def _():
        rank_t = infot_ref[pl.ds(grp, 1), :]
        member_t = infot_ref[N_GROUPS:N_GROUPS + 1, :] == grp_f

        def gather(rb, carry):
            r0 = pl.multiple_of(rb * RB, RB)
            want = (r0 + 1 + lax.broadcasted_iota(jnp.int32, (RB, tm), 0)).astype(F32)
            pick = jnp.where(member_t & (rank_t == want), 1.0, 0.0).astype(MXU_DTYPE)
            xg_ref[pl.ds(r0, RB), :] = jnp.dot(pick, h_ref[...], preferred_element_type=F32).astype(xg_ref.dtype)
            gg_ref[pl.ds(r0, RB), :] = (jnp.dot(pick, gate3_ref[0], preferred_element_type=F32)
                                        + jnp.dot(pick, gate3_ref[1], preferred_element_type=F32)
                                        + jnp.dot(pick, gate3_ref[2], preferred_element_type=F32))
            return carry

        lax.fori_loop(0, n_blk, gather, 0)
        yg_ref[...] = jnp.zeros(yg_ref.shape, F32)

    def expert(rb, carry):
        r0 = pl.multiple_of(rb * RB, RB)
        xb = xg_ref[pl.ds(r0, RB), :]
        gates = gg_ref[pl.ds(r0, RB), :]
        lane = lax.broadcasted_iota(jnp.int32, (RB, LANES), 1)
        y = jnp.zeros((RB, D_MODEL), F32)
        for ee in range(EXPERTS_PER_STEP):
            a = jnp.dot(xb, wg_ref[ee], preferred_element_type=F32)
            b = jnp.dot(xb, wu_ref[ee], preferred_element_type=F32)
            e = step * EXPERTS_PER_STEP + ee
            g_col = jnp.sum(jnp.where(lane == e, gates, 0.0), axis=-1, keepdims=True)
            act = (a * jax.nn.sigmoid(a)) * b * g_col
            y = y + jnp.dot(act.astype(MXU_DTYPE), wd_ref[ee], preferred_element_type=F32)
        yg_ref[pl.ds(r0, RB), :] += y
        return carry

    lax.fori_loop(0, n_blk, expert, 0)

    @pl.when(step % steps_per_group == steps_per_group - 1)
    def _():
        lane = lax.broadcasted_iota(jnp.int32, (tm, LANES), 1)
        info = info_ref[...]
        rank_c = jnp.sum(jnp.where(lane == grp, info, 0.0), axis=-1, keepdims=True)
        member_c = jnp.sum(jnp.where(lane == N_GROUPS, info, 0.0), axis=-1, keepdims=True) == grp_f

        def scatter(sb, carry):
            r0 = pl.multiple_of(sb * SB, SB)
            want = (r0 + 1 + lax.broadcasted_iota(jnp.int32, (tm, SB), 1)).astype(F32)
            put = jnp.where(member_c & (rank_c == want), 1.0, 0.0).astype(MXU_DTYPE)
            y = yg_ref[pl.ds(r0, SB), :]
            y_hi = y.astype(MXU_DTYPE)
            y_lo = (y - y_hi.astype(F32)).astype(MXU_DTYPE)
            o_ref[...] += (jnp.dot(put, y_hi, preferred_element_type=F32)
                           + jnp.dot(put, y_lo, preferred_element_type=F32))
            return carry

        lax.fori_loop(0, (n_rows + (SB - 1)) // SB, scatter, 0)

    if final_norm:
        @pl.when(step == N_EXPERTS // EXPERTS_PER_STEP - 1)
        def _():
            o_ref[...] = _rms(o_ref[...], fg_ref[...])


def _moe(x2d, g2, rgw, rgb, rew, reb, tri, wg, wu, wd, fg, layer, final_norm, tm):
    t = x2d.shape[0]
    cap = pl.cdiv(tm, RB) * RB
    full = lambda i, e: (0, 0)
    return pl.pallas_call(
        functools.partial(_moe_kernel, tm=tm, final_norm=final_norm),
        grid=(t // tm, N_EXPERTS // EXPERTS_PER_STEP),
        in_specs=[
            pl.BlockSpec((tm, D_MODEL), lambda i, e: (i, 0)),
            pl.BlockSpec((1, D_MODEL), full),
            pl.BlockSpec((D_MODEL, LANES), full),
            pl.BlockSpec((1, LANES), full),
            pl.BlockSpec((D_MODEL, LANES), full),
            pl.BlockSpec((1, LANES), full),
            pl.BlockSpec((tm, tm), full, pipeline_mode=pl.Buffered(1)),
            pl.BlockSpec((EXPERTS_PER_STEP, D_MODEL, EXPERT_HIDDEN), lambda i, e: (layer * (N_EXPERTS // EXPERTS_PER_STEP) + e, 0, 0)),
            pl.BlockSpec((EXPERTS_PER_STEP, D_MODEL, EXPERT_HIDDEN), lambda i, e: (layer * (N_EXPERTS // EXPERTS_PER_STEP) + e, 0, 0)),
            pl.BlockSpec((EXPERTS_PER_STEP, EXPERT_HIDDEN, D_MODEL), lambda i, e: (layer * (N_EXPERTS // EXPERTS_PER_STEP) + e, 0, 0)),
            pl.BlockSpec((1, D_MODEL), full),
        ],
        out_specs=pl.BlockSpec((tm, D_MODEL), lambda i, e: (i, 0)),
        out_shape=jax.ShapeDtypeStruct((t, D_MODEL), F32),
        scratch_shapes=[
            pltpu.VMEM((tm, D_MODEL), MXU_DTYPE),
            pltpu.VMEM((3, tm, LANES), MXU_DTYPE),
            pltpu.VMEM((tm, LANES), F32),
            pltpu.VMEM((8, tm), F32),
            pltpu.VMEM((cap, D_MODEL), MXU_DTYPE),
            pltpu.VMEM((cap, LANES), F32),
            pltpu.VMEM((cap, D_MODEL), F32),
            pltpu.SMEM((N_GROUPS,), jnp.int32),
        ],
        compiler_params=pltpu.CompilerParams(
            dimension_semantics=("parallel", "arbitrary"), vmem_limit_bytes=MOE_VMEM_LIMIT),
        name="moe",
    )(x2d, g2, rgw, rgb, rew, reb, tri, wg, wu, wd, fg)


def _pad_lanes(w):
    return jnp.pad(w, ((0, 0), (0, LANES - w.shape[-1])))


def _block_diag(blocks):
    g, n, _ = blocks.shape
    out = jnp.zeros((g * n, g * n), blocks.dtype)
    for i in range(g):
        out = out.at[i * n:(i + 1) * n, i * n:(i + 1) * n].set(blocks[i])
    return out


def kernel(x, norm1_g, w_in, pool_w, pool_scale, dw_w, dw_b, conv_ln_g, conv_ln_b, pw_w, pw_b,
           w_out, norm2_g, rg_w, rg_b, re_w, re_b, w_gate, w_up, w_down, final_g):
    batch, seq, d = x.shape
    depth = w_in.shape[0]
    t = batch * seq
    tm = min(512, seq)
    tm_moe = min(1024, t)
    assert QB % (2 * TK) == 0 and d == D_MODEL and seq % QB == 0 and seq % tm == 0 and tm % TK == 0 and t % tm_moe == 0

    o_q, o_k, o_v = 0, ATT_WIDTH, 2 * ATT_WIDTH
    o_qi = 3 * ATT_WIDTH
    o_ki = o_qi + IDX_HEADS * IDX_HEAD_DIM
    o_wi = o_ki + IDX_HEAD_DIM
    o_pool = o_wi + IDX_HEADS

    wg = w_gate.reshape(depth * N_EXPERTS, D_MODEL, EXPERT_HIDDEN).astype(MXU_DTYPE)
    wu = w_up.reshape(depth * N_EXPERTS, D_MODEL, EXPERT_HIDDEN).astype(MXU_DTYPE)
    wd = w_down.reshape(depth * N_EXPERTS, EXPERT_HIDDEN, D_MODEL).astype(MXU_DTYPE)
    fg = final_g.reshape(1, D_MODEL)
    tri = jnp.tri(tm_moe, dtype=MXU_DTYPE)

    xf = x.reshape(t, D_MODEL)
    for l in range(depth):
        w = w_in[l]
        w_ki = w[:, o_ki:o_wi]
        w_cat = jnp.concatenate([w[:, o_k:o_v], w_ki, w_ki, w[:, o_pool:]], axis=1).astype(MXU_DTYPE)
        wt = jnp.concatenate([w[:, o_q:o_k], w[:, o_qi:o_ki], w[:, o_v:o_qi]], axis=1).T.astype(MXU_DTYPE)
        wwit = jnp.pad(w[:, o_wi:o_pool].T, ((0, WI_ROWS - IDX_HEADS), (0, 0))).astype(MXU_DTYPE)
        o16, o32, qt, vt, wit = _proj(xf, norm1_g[l].reshape(1, D_MODEL), w_cat, wt, wwit, tm)

        ya = _attn(qt, o16, vt, wit, batch, seq)

        x1 = _mix(
            xf, ya, o32,
            _block_diag(pool_w[l]).astype(MXU_DTYPE), pool_scale[l].reshape(1, POOL_WIDTH),
            jnp.pad(dw_w[l], ((0, HALO - CONV_KERNEL), (0, 0))), dw_b[l].reshape(1, CONV_WIDTH),
            conv_ln_g[l].reshape(1, CONV_WIDTH), conv_ln_b[l].reshape(1, CONV_WIDTH),
            pw_w[l].astype(MXU_DTYPE), pw_b[l].reshape(1, CONV_WIDTH),
            w_out[l].astype(MXU_DTYPE), batch, seq, tm)

        xf = _moe(
            x1, norm2_g[l].reshape(1, D_MODEL),
            _pad_lanes(rg_w[l]).astype(MXU_DTYPE), _pad_lanes(rg_b[l].reshape(1, N_GROUPS)),
            _pad_lanes(re_w[l]).astype(MXU_DTYPE), _pad_lanes(re_b[l].reshape(1, N_EXPERTS)),
            tri, wg, wu, wd, fg, l, l == depth - 1, tm_moe)
    return xf.reshape(batch, seq, D_MODEL)
```

```python
import functools

import jax
import jax.numpy as jnp
from jax import lax
from jax.experimental import pallas as pl
from jax.experimental.pallas import tpu as pltpu

F32 = jnp.float32
MXU_DTYPE = jnp.bfloat16
COARSE_DTYPE = jnp.bfloat16

D_MODEL = 1024
CHUNK = 64
ATT_HEADS = 8
ATT_HEAD_DIM = 64
ATT_WIDTH = ATT_HEADS * ATT_HEAD_DIM
IDX_HEADS = 8
IDX_HEAD_DIM = 64
TOPK_MAX = 256
POOL_GROUPS = 4
POOL_GROUP_DIM = 64
POOL_WIDTH = POOL_GROUPS * POOL_GROUP_DIM
POOL_WINDOWS = (2, 4, 8, 16)
CONV_WIDTH = 256
CONV_KERNEL = 31
N_GROUPS = 4
EXPERTS_PER_GROUP = 4
N_EXPERTS = N_GROUPS * EXPERTS_PER_GROUP
EXPERT_HIDDEN = 512
EPS = 1e-6

LANES = 128
SUBLANES = 8
INT_MIN = -2 ** 31
NEG_INF = float("-inf")

N16 = ATT_WIDTH + 2 * IDX_HEAD_DIM
KI_COL_BLOCK = ATT_WIDTH // LANES
N32 = POOL_WIDTH + 2 * CONV_WIDTH
NT_ROWS = 3 * ATT_WIDTH
WI_ROWS = 16
VT_ROWS = ATT_HEAD_DIM + 16
KEY_NEG_INF = (0xFF800000 ^ 0x7FFFFFFF) - 2 ** 32

QB = 512
TK = 256
AT = 2 * TK
HEAD_GROUP = 8
SEARCH_COLS = 256
CNT16_ROWS = 32
FINE_BITS = 17
CNT_ROWS = 16
HALO = 32
VMEM_LIMIT = 48 * 1024 * 1024
MOE_VMEM_LIMIT = 58 * 1024 * 1024
RB = 288
EXPERTS_PER_STEP = 4
SB = 256

_NT = (((1,), (1,)), ((), ()))


def _rms(x, g):
    return x * lax.rsqrt(jnp.mean(x * x, axis=-1, keepdims=True) + EPS) * g


def _proj_kernel(x_ref, g_ref, w_ref, wt_ref, wwit_ref, o16_ref, o32_ref, qt_ref, vt_ref, wit_ref, *, tm):
    h = _rms(x_ref[...], g_ref[...]).astype(MXU_DTYPE)
    p = jnp.dot(h, w_ref[...], preferred_element_type=F32)
    o16_ref[...] = p[:, :N16].astype(o16_ref.dtype)
    o32_ref[...] = p[:, N16:]
    pt = lax.dot_general(wt_ref[...], h, _NT, preferred_element_type=F32)
    qt_ref[...] = pt[:2 * ATT_WIDTH, :].astype(qt_ref.dtype)
    ones = jnp.ones((VT_ROWS - ATT_HEAD_DIM, TK), vt_ref.dtype)
    for c in range(tm // TK):
        for hd in range(ATT_HEADS):
            r0 = 2 * ATT_WIDTH + hd * ATT_HEAD_DIM
            vt_ref[c, hd * VT_ROWS:hd * VT_ROWS + ATT_HEAD_DIM, :] = (
                pt[r0:r0 + ATT_HEAD_DIM, c * TK:(c + 1) * TK].astype(vt_ref.dtype))
            vt_ref[c, hd * VT_ROWS + ATT_HEAD_DIM:(hd + 1) * VT_ROWS, :] = ones
    wit_ref[...] = lax.dot_general(wwit_ref[...], h, _NT, preferred_element_type=F32)


def _proj(x2d, g, w, wt, wwit, tm):
    t = x2d.shape[0]
    return pl.pallas_call(
        functools.partial(_proj_kernel, tm=tm),
        grid=(t // tm,),
        in_specs=[
            pl.BlockSpec((tm, D_MODEL), lambda i: (i, 0)),
            pl.BlockSpec((1, D_MODEL), lambda i: (0, 0)),
            pl.BlockSpec((D_MODEL, N16 + N32), lambda i: (0, 0)),
            pl.BlockSpec((NT_ROWS, D_MODEL), lambda i: (0, 0)),
            pl.BlockSpec((WI_ROWS, D_MODEL), lambda i: (0, 0)),
        ],
        out_specs=[
            pl.BlockSpec((tm, N16), lambda i: (i, 0)),
            pl.BlockSpec((tm, N32), lambda i: (i, 0)),
            pl.BlockSpec((2 * ATT_WIDTH, tm), lambda i: (0, i)),
            pl.BlockSpec((tm // TK, ATT_HEADS * VT_ROWS, TK), lambda i: (i, 0, 0)),
            pl.BlockSpec((WI_ROWS, tm), lambda i: (0, i)),
        ],
        out_shape=[
            jax.ShapeDtypeStruct((t, N16), MXU_DTYPE),
            jax.ShapeDtypeStruct((t, N32), F32),
            jax.ShapeDtypeStruct((2 * ATT_WIDTH, t), MXU_DTYPE),
            jax.ShapeDtypeStruct((t // TK, ATT_HEADS * VT_ROWS, TK), MXU_DTYPE),
            jax.ShapeDtypeStruct((WI_ROWS, t), F32),
        ],
        compiler_params=pltpu.CompilerParams(
            dimension_semantics=("parallel",), vmem_limit_bytes=VMEM_LIMIT),
        name="proj",
    )(x2d, g, w, wt, wwit)


def _attn_kernel(qt_ref, qit_ref, k_ref, ki_ref, vt_ref, wit_ref, o_ref,
                 qm_ref, qim_ref, tri_ref, sc_ref, sc16_ref, lg_ref, p_ref, acc_ref, m_ref, l_ref, *, ktop):
    j = pl.program_id(1)
    nkt = (j + 1) * (QB // TK)

    row = lax.broadcasted_iota(jnp.int32, (LANES, QB), 0)
    for h in range(ATT_HEADS):
        pr, half = divmod(h, 2)
        keep = (row < ATT_HEAD_DIM) if half == 0 else (row >= ATT_HEAD_DIM)
        qp = qt_ref[pr * LANES:(pr + 1) * LANES, :].astype(F32) * (ATT_HEAD_DIM ** -0.5)
        qm_ref[h] = jnp.where(keep, qp, 0.0).astype(qm_ref.dtype)
        qip = qit_ref[pr * LANES:(pr + 1) * LANES, :].astype(F32) * (IDX_HEAD_DIM ** -0.5)
        qim_ref[h] = jnp.where(keep, qip, 0.0).astype(qim_ref.dtype)

    r_i = lax.broadcasted_iota(jnp.int32, (AT, AT), 0)
    c_i = lax.broadcasted_iota(jnp.int32, (AT, AT), 1)
    tri_ref[...] = jnp.where(c_i <= r_i, 1.0, 0.0).astype(tri_ref.dtype)

    wt = wit_ref[...]
    q_chunk = (j * QB + lax.broadcasted_iota(jnp.int32, (1, QB), 1)) // CHUNK

    def admissible(k0, rows):
        k_chunk = (k0 + lax.broadcasted_iota(jnp.int32, (rows, 1), 0)) // CHUNK
        return k_chunk <= q_chunk

    def score_slab(kp, carry):
        k0 = pl.multiple_of(kp * (2 * TK), 2 * TK)
        kit = ki_ref[pl.ds(k0, 2 * TK), :]
        acc = jnp.zeros((2 * TK, QB), F32)
        for h in range(IDX_HEADS):
            d = jnp.dot(kit, qim_ref[h], preferred_element_type=F32)
            acc = acc + jnp.maximum(d, 0.0) * wt[h:h + 1, :]
        score = jnp.where(admissible(k0, 2 * TK), acc * (IDX_HEADS ** -0.5), NEG_INF)
        sc_ref[kp] = score
        sc16_ref[kp] = score.astype(sc16_ref.dtype)
        return carry

    lax.fori_loop(0, nkt // 2, score_slab, 0)

    n_slab = nkt // 2

    def key_to_float(key):
        bits = key ^ ((key >> 31) & 0x7FFFFFFF)
        return jnp.where(key < KEY_NEG_INF, NEG_INF, lax.bitcast_convert_type(bits, F32))

    def coarse_key(u):
        key = lax.shift_left(u, 16) ^ INT_MIN
        return jnp.where(key < 0, key | 0xFFFF, key)

    def count(pred_fn, c0):
        def body(kp, cnt):
            ones = jnp.where(pred_fn(sc_ref[kp, :, c0:c0 + SEARCH_COLS]), 1.0, 0.0)
            return cnt + jnp.sum(ones.reshape(2 * TK // CNT_ROWS, CNT_ROWS, SEARCH_COLS), axis=0)
        cnt = lax.fori_loop(0, n_slab, body, jnp.zeros((CNT_ROWS, SEARCH_COLS), F32))
        return jnp.sum(cnt, axis=0, keepdims=True)

    def count16(cand, c0):
        one = jnp.ones((), sc16_ref.dtype)
        zero = jnp.zeros((), sc16_ref.dtype)

        def body(kp, cnt):
            ones = jnp.where(sc16_ref[kp, :, c0:c0 + SEARCH_COLS] >= cand, one, zero)
            parts = [ones[r:r + CNT16_ROWS, :] for r in range(0, 2 * TK, CNT16_ROWS)]
            while len(parts) > 1:
                parts = [a + b for a, b in zip(parts[0::2], parts[1::2])]
            return cnt + parts[0].astype(F32)
        cnt = lax.fori_loop(0, n_slab, body, jnp.zeros((CNT16_ROWS, SEARCH_COLS), F32))
        return jnp.sum(cnt, axis=0, keepdims=True)

    thr_parts, need_parts = [], []
    for c0 in range(0, QB, SEARCH_COLS):
        def coarse_step(i, prefix, c0=c0):
            cand_u = prefix | lax.shift_left(jnp.int32(1), 15 - i)
            cand = key_to_float(coarse_key(cand_u)).astype(sc16_ref.dtype)
            return jnp.where(count16(cand, c0) >= ktop, cand_u, prefix)

        lead = lax.fori_loop(0, 16, coarse_step, jnp.zeros((1, SEARCH_COLS), jnp.int32))
        base = jnp.maximum(coarse_key(lead) - (1 << 15), KEY_NEG_INF)

        def fine_step(i, off, c0=c0, base=base):
            cand_off = off | lax.shift_left(jnp.int32(1), FINE_BITS - 1 - i)
            cand = key_to_float(base + cand_off)
            return jnp.where(count(lambda s: s >= cand, c0) >= ktop, cand_off, off)

        off = lax.fori_loop(0, FINE_BITS, fine_step, jnp.zeros((1, SEARCH_COLS), jnp.int32))
        thr_c = key_to_float(base + off)
        thr_parts.append(thr_c)
        need_parts.append(ktop - count(lambda s: s > thr_c, c0))
    thr = jnp.concatenate(thr_parts, axis=1)
    need = jnp.concatenate(need_parts, axis=1)

    m_ref[...] = jnp.full(m_ref.shape, NEG_INF, F32)
    l_ref[...] = jnp.zeros(l_ref.shape, F32)
    acc_ref[...] = jnp.zeros(acc_ref.shape, F32)

    def attend_slab(kp, eq_before):
        k0 = pl.multiple_of(kp * AT, AT)
        s = sc_ref[kp]
        eq = s == thr
        eqf = jnp.where(eq, 1.0, 0.0)
        incl = jnp.dot(tri_ref[...], eqf.astype(tri_ref.dtype), preferred_element_type=F32)
        sel = ((s > thr) | (eq & ((eq_before + incl) <= need))) & admissible(k0, AT)
        bias = jnp.where(sel, 0.0, NEG_INF)
        reread = jnp.minimum(kp, 0)
        alphas = {}
        for h in range(ATT_HEADS):
            pr = h // 2
            kpair = k_ref[pl.ds(k0, AT), pr * LANES:(pr + 1) * LANES]
            lg = jnp.dot(kpair, qm_ref[h], preferred_element_type=F32) + bias
            lg_ref[h] = lg
            m_old = m_ref[h:h + 1, :]
            m_tile = jnp.max(lg.reshape(AT // CNT_ROWS, CNT_ROWS, QB), axis=0)
            m_new = jnp.maximum(m_old, jnp.max(m_tile, axis=0, keepdims=True))
            m_ref[h:h + 1, :] = m_new
            m_safe = jnp.where(m_new == NEG_INF, 0.0, m_new)
            alphas[h] = (jnp.exp(m_old - m_safe), m_safe)
        for h in range(ATT_HEADS):
            lg = lg_ref[h + reread]
            p_ref[h] = jnp.exp(lg - alphas[h][1]).astype(p_ref.dtype)
        for h in range(ATT_HEADS):
            alpha = alphas[h][0]
            vrows = slice(h * VT_ROWS, (h + 1) * VT_ROWS)
            pv = (jnp.dot(vt_ref[2 * kp, vrows, :], p_ref[h, :TK, :], preferred_element_type=F32)
                  + jnp.dot(vt_ref[2 * kp + 1, vrows, :], p_ref[h, TK:, :], preferred_element_type=F32))
            rows = slice(h * ATT_HEAD_DIM, (h + 1) * ATT_HEAD_DIM)
            acc_ref[rows, :] = alpha * acc_ref[rows, :] + pv[:ATT_HEAD_DIM, :]
            l_ref[h:h + 1, :] = alpha * l_ref[h:h + 1, :] + pv[ATT_HEAD_DIM:ATT_HEAD_DIM + 1, :]
        return eq_before + jnp.sum(eqf, axis=0, keepdims=True)

    lax.fori_loop(0, nkt // 2, attend_slab, jnp.zeros((1, QB), F32))

    for h in range(ATT_HEADS):
        rows = slice(h * ATT_HEAD_DIM, (h + 1) * ATT_HEAD_DIM)
        acc_ref[rows, :] = acc_ref[rows, :] / l_ref[h:h + 1, :]
    o_ref[...] = acc_ref[...].T.astype(o_ref.dtype)


def _attn(qt, o16, vt, wit, batch, seq):
    t = batch * seq
    nq = seq // QB
    nkt = seq // TK
    ktop = min(TOPK_MAX, seq // 4)
    return pl.pallas_call(
        functools.partial(_attn_kernel, ktop=ktop),
        grid=(batch, nq),
        in_specs=[
            pl.BlockSpec((ATT_WIDTH, QB), lambda b, j: (0, b * nq + j)),
            pl.BlockSpec((ATT_WIDTH, QB), lambda b, j: (1, b * nq + j)),
            pl.BlockSpec((seq, ATT_WIDTH), lambda b, j: (b, 0)),
            pl.BlockSpec((seq, LANES), lambda b, j: (b, KI_COL_BLOCK)),
            pl.BlockSpec((nkt, ATT_HEADS * VT_ROWS, TK), lambda b, j: (b, 0, 0)),
            pl.BlockSpec((WI_ROWS, QB), lambda b, j: (0, b * nq + j)),
        ],
        out_specs=pl.BlockSpec((QB, ATT_WIDTH), lambda b, j: (b * nq + j, 0)),
        out_shape=jax.ShapeDtypeStruct((t, ATT_WIDTH), MXU_DTYPE),
        scratch_shapes=[
            pltpu.VMEM((ATT_HEADS, LANES, QB), MXU_DTYPE),
            pltpu.VMEM((IDX_HEADS, LANES, QB), MXU_DTYPE),
            pltpu.VMEM((AT, AT), MXU_DTYPE),
            pltpu.VMEM(((nkt + 1) // 2, 2 * TK, QB), F32),
            pltpu.VMEM(((nkt + 1) // 2, 2 * TK, QB), COARSE_DTYPE),
            pltpu.VMEM((ATT_HEADS, AT, QB), F32),
            pltpu.VMEM((ATT_HEADS, AT, QB), MXU_DTYPE),
            pltpu.VMEM((ATT_WIDTH, QB), F32),
            pltpu.VMEM((ATT_HEADS, QB), F32),
            pltpu.VMEM((ATT_HEADS, QB), F32),
        ],
        compiler_params=pltpu.CompilerParams(
            dimension_semantics=("parallel", "parallel"), vmem_limit_bytes=MOE_VMEM_LIMIT),
        name="attn",
    )(qt, qt, o16, o16, vt, wit)


def _mix_kernel(x_ref, ya_ref, cur_ref, halo_ref, wp_ref, ps_ref, dw_ref, dwb_ref, lng_ref, lnb_ref,
                pw_ref, pwb_ref, wo_ref, o_ref, ubuf, hbuf, hsh, *, tm):
    j = pl.program_id(1)
    cur = cur_ref[...]
    halo = jnp.where(j > 0, halo_ref[...], 0.0)

    def glu(z):
        return z[:, POOL_WIDTH:POOL_WIDTH + CONV_WIDTH] * jax.nn.sigmoid(z[:, POOL_WIDTH + CONV_WIDTH:])

    u = cur[:, :POOL_WIDTH]
    ubuf[0:HALO, :] = halo[:, :POOL_WIDTH]
    ubuf[HALO:, :] = u
    hbuf[0:HALO, :] = glu(halo)
    hbuf[HALO:, :] = glu(cur)

    lane = lax.broadcasted_iota(jnp.int32, (tm, LANES), 1)
    upper = lane >= POOL_GROUP_DIM
    s0 = u[:, :LANES]
    s1 = u[:, LANES:]
    for i in range(1, POOL_WINDOWS[3]):
        if i < POOL_WINDOWS[1]:
            sh = ubuf[HALO - i:HALO - i + tm, 0:LANES]
            s0 = s0 + (sh if i < POOL_WINDOWS[0] else jnp.where(upper, sh, 0.0))
        sh = ubuf[HALO - i:HALO - i + tm, LANES:2 * LANES]
        s1 = s1 + (sh if i < POOL_WINDOWS[2] else jnp.where(upper, sh, 0.0))
    t1 = (j * tm + lax.broadcasted_iota(jnp.int32, (tm, LANES), 0) + 1).astype(F32)
    w0 = jnp.where(upper, float(POOL_WINDOWS[1]), float(POOL_WINDOWS[0]))
    w1 = jnp.where(upper, float(POOL_WINDOWS[3]), float(POOL_WINDOWS[2]))
    pooled = jnp.concatenate([s0 / jnp.minimum(t1, w0), s1 / jnp.minimum(t1, w1)], axis=1)
    d = (pooled - u).astype(MXU_DTYPE)
    yb = jnp.dot(d, wp_ref[...], preferred_element_type=F32) * ps_ref[...]

    span = tm + HALO - SUBLANES
    for ph in range(1, SUBLANES):
        hsh[ph - 1] = hbuf[ph:ph + span, :]
    c = jnp.zeros((tm, CONV_WIDTH), F32) + dwb_ref[...]
    off = HALO - (CONV_KERNEL - 1)
    for jj in range(CONV_KERNEL):
        a, ph = divmod(off + jj, SUBLANES)
        rows = slice(a * SUBLANES, a * SUBLANES + tm)
        tap = hbuf[rows, :] if ph == 0 else hsh[ph - 1, rows, :]
        c = c + tap * dw_ref[jj:jj + 1, :]
    mu = jnp.mean(c, axis=-1, keepdims=True)
    cc = c - mu
    var = jnp.mean(cc * cc, axis=-1, keepdims=True)
    hn = cc * lax.rsqrt(var + EPS) * lng_ref[...] + lnb_ref[...]
    sw = (hn * jax.nn.sigmoid(hn)).astype(MXU_DTYPE)
    yc = jnp.dot(sw, pw_ref[...], preferred_element_type=F32) + pwb_ref[...]

    y = jnp.dot(ya_ref[...], wo_ref[0:ATT_WIDTH, :], preferred_element_type=F32)
    y = y + jnp.dot(yb.astype(MXU_DTYPE), wo_ref[ATT_WIDTH:ATT_WIDTH + POOL_WIDTH, :], preferred_element_type=F32)
    y = y + jnp.dot(yc.astype(MXU_DTYPE), wo_ref[ATT_WIDTH + POOL_WIDTH:, :], preferred_element_type=F32)
    o_ref[...] = x_ref[...] + y


def _mix(x2d, ya, o32, wp, ps, dw, dwb, lng, lnb, pw, pwb, wo, batch, seq, tm):
    t = batch * seq
    nt = seq // tm
    hb = tm // HALO
    full = lambda b, j: (0, 0)
    return pl.pallas_call(
        functools.partial(_mix_kernel, tm=tm),
        grid=(batch, nt),
        in_specs=[
            pl.BlockSpec((tm, D_MODEL), lambda b, j: (b * nt + j, 0)),
            pl.BlockSpec((tm, ATT_WIDTH), lambda b, j: (b * nt + j, 0)),
            pl.BlockSpec((tm, N32), lambda b, j: (b * nt + j, 0)),
            pl.BlockSpec((HALO, N32), lambda b, j: (jnp.maximum((b * nt + j) * hb - 1, 0), 0)),
            pl.BlockSpec((POOL_WIDTH, POOL_WIDTH), full),
            pl.BlockSpec((1, POOL_WIDTH), full),
            pl.BlockSpec((HALO, CONV_WIDTH), full),
            pl.BlockSpec((1, CONV_WIDTH), full),
            pl.BlockSpec((1, CONV_WIDTH), full),
            pl.BlockSpec((1, CONV_WIDTH), full),
            pl.BlockSpec((CONV_WIDTH, CONV_WIDTH), full),
            pl.BlockSpec((1, CONV_WIDTH), full),
            pl.BlockSpec((D_MODEL, D_MODEL), full),
        ],
        out_specs=pl.BlockSpec((tm, D_MODEL), lambda b, j: (b * nt + j, 0)),
        out_shape=jax.ShapeDtypeStruct((t, D_MODEL), F32),
        scratch_shapes=[
            pltpu.VMEM((HALO + tm, POOL_WIDTH), F32),
            pltpu.VMEM((HALO + tm, CONV_WIDTH), F32),
            pltpu.VMEM((SUBLANES - 1, HALO + tm - SUBLANES, CONV_WIDTH), F32),
        ],
        compiler_params=pltpu.CompilerParams(
            dimension_semantics=("parallel", "parallel"), vmem_limit_bytes=VMEM_LIMIT),
        name="mix",
    )(x2d, ya, o32, o32, wp, ps, dw, dwb, lng, lnb, pw, pwb, wo)


def _route(glog, elog):
    lane = lax.broadcasted_iota(jnp.int32, glog.shape, 1)
    lane_f = lane.astype(F32)
    big = float(LANES)
    gl = jnp.where(lane < N_GROUPS, glog, NEG_INF)
    ge = jnp.exp(gl - jnp.max(gl, axis=-1, keepdims=True))
    gp = ge / jnp.sum(ge, axis=-1, keepdims=True)
    p_g = jnp.max(gp, axis=-1, keepdims=True)
    g_sel = jnp.min(jnp.where(gp == p_g, lane_f, big), axis=-1, keepdims=True)
    in_grp = (lane // EXPERTS_PER_GROUP).astype(F32) == g_sel
    el = jnp.where(in_grp, elog, NEG_INF)
    ee = jnp.exp(el - jnp.max(el, axis=-1, keepdims=True))
    ep = ee / jnp.sum(ee, axis=-1, keepdims=True)
    ep = jnp.where(in_grp, ep, -1.0)
    v1 = jnp.max(ep, axis=-1, keepdims=True)
    i1 = jnp.min(jnp.where(ep == v1, lane_f, big), axis=-1, keepdims=True)
    ep2 = jnp.where(lane_f == i1, -1.0, ep)
    v2 = jnp.max(ep2, axis=-1, keepdims=True)
    i2 = jnp.min(jnp.where(ep2 == v2, lane_f, big), axis=-1, keepdims=True)
    den = v1 + v2
    w_e = jnp.where(lane_f == i1, v1 / den, jnp.where(lane_f == i2, v2 / den, 0.0))
    return p_g * w_e, g_sel


def _moe_kernel(x_ref, g2_ref, rgw_ref, rgb_ref, rew_ref, reb_ref, tri_ref, wg_ref, wu_ref, wd_ref, fg_ref,
                o_ref, h_ref, gate3_ref, info_ref, infot_ref, xg_ref, gg_ref, yg_ref, cnt_ref,
                *, tm, final_norm):
    step = pl.program_id(1)
    steps_per_group = EXPERTS_PER_GROUP // EXPERTS_PER_STEP
    grp = step // steps_per_group
    grp_f = grp.astype(F32)

    @pl.when(step == 0)
    def _():
        x = x_ref[...]
        hb = _rms(x, g2_ref[...]).astype(MXU_DTYPE)
        h_ref[...] = hb
        glog = jnp.dot(hb, rgw_ref[...], preferred_element_type=F32) + rgb_ref[...]
        elog = jnp.dot(hb, rew_ref[...], preferred_element_type=F32) + reb_ref[...]
        gate, g_sel = _route(glog, elog)
        g1 = gate.astype(MXU_DTYPE)
        r1 = gate - g1.astype(F32)
        g2 = r1.astype(MXU_DTYPE)
        gate3_ref[0] = g1
        gate3_ref[1] = g2
        gate3_ref[2] = (r1 - g2.astype(F32)).astype(MXU_DTYPE)
        lane = lax.broadcasted_iota(jnp.int32, (tm, LANES), 1)
        member = jnp.where(lane.astype(F32) == g_sel, 1.0, 0.0).astype(MXU_DTYPE)
        ranks = jnp.dot(tri_ref[...], member, preferred_element_type=F32)
        info = jnp.where(lane < N_GROUPS, ranks, jnp.where(lane == N_GROUPS, g_sel, 0.0))
        info_ref[...] = info
        infot_ref[...] = info.T[:8, :]
        for gi in range(N_GROUPS):
            cnt_ref[gi] = ranks[tm - 1, gi].astype(jnp.int32)
        o_ref[...] = x

    n_rows = cnt_ref[grp]
    n_blk = (n_rows + (RB - 1)) // RB

    @pl.when(step % steps_per_group == 0)
    def _():
        rank_t = infot_ref[pl.ds(grp, 1), :]
        member_t = infot_ref[N_GROUPS:N_GROUPS + 1, :] == grp_f

        def gather(rb, carry):
            r0 = pl.multiple_of(rb * RB, RB)
            want = (r0 + 1 + lax.broadcasted_iota(jnp.int32, (RB, tm), 0)).astype(F32)
            pick = jnp.where(member_t & (rank_t == want), 1.0, 0.0).astype(MXU_DTYPE)
            xg_ref[pl.ds(r0, RB), :] = jnp.dot(pick, h_ref[...], preferred_element_type=F32).astype(xg_ref.dtype)
            gg_ref[pl.ds(r0, RB), :] = (jnp.dot(pick, gate3_ref[0], preferred_element_type=F32)
                                        + jnp.dot(pick, gate3_ref[1], preferred_element_type=F32)
                                        + jnp.dot(pick, gate3_ref[2], preferred_element_type=F32))
            return carry

        lax.fori_loop(0, n_blk, gather, 0)
        yg_ref[...] = jnp.zeros(yg_ref.shape, F32)

    def expert(rb, carry):
        r0 = pl.multiple_of(rb * RB, RB)
        xb = xg_ref[pl.ds(r0, RB), :]
        gates = gg_ref[pl.ds(r0, RB), :]
        lane = lax.broadcasted_iota(jnp.int32, (RB, LANES), 1)
        y = jnp.zeros((RB, D_MODEL), F32)
        for ee in range(EXPERTS_PER_STEP):
            a = jnp.dot(xb, wg_ref[ee], preferred_element_type=F32)
            b = jnp.dot(xb, wu_ref[ee], preferred_element_type=F32)
            e = step * EXPERTS_PER_STEP + ee
            g_col = jnp.sum(jnp.where(lane == e, gates, 0.0), axis=-1, keepdims=True)
            act = (a * jax.nn.sigmoid(a)) * b * g_col
            y = y + jnp.dot(act.astype(MXU_DTYPE), wd_ref[ee], preferred_element_type=F32)
        yg_ref[pl.ds(r0, RB), :] += y
        return carry

    lax.fori_loop(0, n_blk, expert, 0)

    @pl.when(step % steps_per_group == steps_per_group - 1)
    def _():
        lane = lax.broadcasted_iota(jnp.int32, (tm, LANES), 1)
        info = info_ref[...]
        rank_c = jnp.sum(jnp.where(lane == grp, info, 0.0), axis=-1, keepdims=True)
        member_c = jnp.sum(jnp.where(lane == N_GROUPS, info, 0.0), axis=-1, keepdims=True) == grp_f

        def scatter(sb, carry):
            r0 = pl.multiple_of(sb * SB, SB)
            want = (r0 + 1 + lax.broadcasted_iota(jnp.int32, (tm, SB), 1)).astype(F32)
            put = jnp.where(member_c & (rank_c == want), 1.0, 0.0).astype(MXU_DTYPE)
            y = yg_ref[pl.ds(r0, SB), :]
            y_hi = y.astype(MXU_DTYPE)
            y_lo = (y - y_hi.astype(F32)).astype(MXU_DTYPE)
            o_ref[...] += (jnp.dot(put, y_hi, preferred_element_type=F32)
                           + jnp.dot(put, y_lo, preferred_element_type=F32))
            return carry

        lax.fori_loop(0, (n_rows + (SB - 1)) // SB, scatter, 0)

    if final_norm:
        @pl.when(step == N_EXPERTS // EXPERTS_PER_STEP - 1)
        def _():
            o_ref[...] = _rms(o_ref[...], fg_ref[...])


def _moe(x2d, g2, rgw, rgb, rew, reb, tri, wg, wu, wd, fg, layer, final_norm, tm):
    t = x2d.shape[0]
    cap = pl.cdiv(tm, RB) * RB
    full = lambda i, e: (0, 0)
    return pl.pallas_call(
        functools.partial(_moe_kernel, tm=tm, final_norm=final_norm),
        grid=(t // tm, N_EXPERTS // EXPERTS_PER_STEP),
        in_specs=[
            pl.BlockSpec((tm, D_MODEL), lambda i, e: (i, 0)),
            pl.BlockSpec((1, D_MODEL), full),
            pl.BlockSpec((D_MODEL, LANES), full),
            pl.BlockSpec((1, LANES), full),
            pl.BlockSpec((D_MODEL, LANES), full),
            pl.BlockSpec((1, LANES), full),
            pl.BlockSpec((tm, tm), full, pipeline_mode=pl.Buffered(1)),
            pl.BlockSpec((EXPERTS_PER_STEP, D_MODEL, EXPERT_HIDDEN), lambda i, e: (layer * (N_EXPERTS // EXPERTS_PER_STEP) + e, 0, 0)),
            pl.BlockSpec((EXPERTS_PER_STEP, D_MODEL, EXPERT_HIDDEN), lambda i, e: (layer * (N_EXPERTS // EXPERTS_PER_STEP) + e, 0, 0)),
            pl.BlockSpec((EXPERTS_PER_STEP, EXPERT_HIDDEN, D_MODEL), lambda i, e: (layer * (N_EXPERTS // EXPERTS_PER_STEP) + e, 0, 0)),
            pl.BlockSpec((1, D_MODEL), full),
        ],
        out_specs=pl.BlockSpec((tm, D_MODEL), lambda i, e: (i, 0)),
        out_shape=jax.ShapeDtypeStruct((t, D_MODEL), F32),
        scratch_shapes=[
            pltpu.VMEM((tm, D_MODEL), MXU_DTYPE),
            pltpu.VMEM((3, tm, LANES), MXU_DTYPE),
            pltpu.VMEM((tm, LANES), F32),
            pltpu.VMEM((8, tm), F32),
            pltpu.VMEM((cap, D_MODEL), MXU_DTYPE),
            pltpu.VMEM((cap, LANES), F32),
            pltpu.VMEM((cap, D_MODEL), F32),
            pltpu.SMEM((N_GROUPS,), jnp.int32),
        ],
        compiler_params=pltpu.CompilerParams(
            dimension_semantics=("parallel", "arbitrary"), vmem_limit_bytes=MOE_VMEM_LIMIT),
        name="moe",
    )(x2d, g2, rgw, rgb, rew, reb, tri, wg, wu, wd, fg)


def _pad_lanes(w):
    return jnp.pad(w, ((0, 0), (0, LANES - w.shape[-1])))


def _block_diag(blocks):
    g, n, _ = blocks.shape
    out = jnp.zeros((g * n, g * n), blocks.dtype)
    for i in range(g):
        out = out.at[i * n:(i + 1) * n, i * n:(i + 1) * n].set(blocks[i])
    return out


def kernel(x, norm1_g, w_in, pool_w, pool_scale, dw_w, dw_b, conv_ln_g, conv_ln_b, pw_w, pw_b,
           w_out, norm2_g, rg_w, rg_b, re_w, re_b, w_gate, w_up, w_down, final_g):
    batch, seq, d = x.shape
    depth = w_in.shape[0]
    t = batch * seq
    tm = min(512, seq)
    tm_moe = min(1024, t)
    assert QB % (2 * TK) == 0 and d == D_MODEL and seq % QB == 0 and seq % tm == 0 and tm % TK == 0 and t % tm_moe == 0

    o_q, o_k, o_v = 0, ATT_WIDTH, 2 * ATT_WIDTH
    o_qi = 3 * ATT_WIDTH
    o_ki = o_qi + IDX_HEADS * IDX_HEAD_DIM
    o_wi = o_ki + IDX_HEAD_DIM
    o_pool = o_wi + IDX_HEADS

    wg = w_gate.reshape(depth * N_EXPERTS, D_MODEL, EXPERT_HIDDEN).astype(MXU_DTYPE)
    wu = w_up.reshape(depth * N_EXPERTS, D_MODEL, EXPERT_HIDDEN).astype(MXU_DTYPE)
    wd = w_down.reshape(depth * N_EXPERTS, EXPERT_HIDDEN, D_MODEL).astype(MXU_DTYPE)
    fg = final_g.reshape(1, D_MODEL)
    tri = jnp.tri(tm_moe, dtype=MXU_DTYPE)

    xf = x.reshape(t, D_MODEL)
    for l in range(depth):
        w = w_in[l]
        w_ki = w[:, o_ki:o_wi]
        w_cat = jnp.concatenate([w[:, o_k:o_v], w_ki, w_ki, w[:, o_pool:]], axis=1).astype(MXU_DTYPE)
        wt = jnp.concatenate([w[:, o_q:o_k], w[:, o_qi:o_ki], w[:, o_v:o_qi]], axis=1).T.astype(MXU_DTYPE)
        wwit = jnp.pad(w[:, o_wi:o_pool].T, ((0, WI_ROWS - IDX_HEADS), (0, 0))).astype(MXU_DTYPE)
        o16, o32, qt, vt, wit = _proj(xf, norm1_g[l].reshape(1, D_MODEL), w_cat, wt, wwit, tm)

        ya = _attn(qt, o16, vt, wit, batch, seq)

        x1 = _mix(
            xf, ya, o32,
            _block_diag(pool_w[l]).astype(MXU_DTYPE), pool_scale[l].reshape(1, POOL_WIDTH),
            jnp.pad(dw_w[l], ((0, HALO - CONV_KERNEL), (0, 0))), dw_b[l].reshape(1, CONV_WIDTH),
            conv_ln_g[l].reshape(1, CONV_WIDTH), conv_ln_b[l].reshape(1, CONV_WIDTH),
            pw_w[l].astype(MXU_DTYPE), pw_b[l].reshape(1, CONV_WIDTH),
            w_out[l].astype(MXU_DTYPE), batch, seq, tm)

        xf = _moe(
            x1, norm2_g[l].reshape(1, D_MODEL),
            _pad_lanes(rg_w[l]).astype(MXU_DTYPE), _pad_lanes(rg_b[l].reshape(1, N_GROUPS)),
            _pad_lanes(re_w[l]).astype(MXU_DTYPE), _pad_lanes(re_b[l].reshape(1, N_EXPERTS)),
            tri, wg, wu, wd, fg, l, l == depth - 1, tm_moe)
    return xf.reshape(batch, seq, D_MODEL)
```
